```python
import jax, jax.numpy as jnp
from jax import lax
import numpy as np

D_MODEL = 1024
BATCH = 2
SEQ = 8192
DEPTH = 1

N_GROUPS = 3
HEADS_PER_GROUP = 4
HEAD_DIM = 128
DILATED_CONFIGS = ((128, 1), (512, 4), (2048, 16))
ATTN_WIDTH = N_GROUPS * HEADS_PER_GROUP * HEAD_DIM
ATTN_OUT = HEADS_PER_GROUP * HEAD_DIM
CONV_WIDTH = D_MODEL
CONV_K = 3
D_FF = 2816
EPS = 1e-6
N_MOD = 9
IN_SIZES = (ATTN_WIDTH, ATTN_WIDTH, ATTN_WIDTH, CONV_WIDTH, CONV_WIDTH, CONV_WIDTH, D_MODEL, D_MODEL)
IN_WIDTH = sum(IN_SIZES)
IN_SPLITS = tuple(int(s) for s in np.cumsum(IN_SIZES)[:-1])

kernel_name = "hybrid_dilated_attn_shortconv_macaron_adaln"


def rms_norm(x, g):
    x32 = x.astype(jnp.float32)
    y = x32 * lax.rsqrt(jnp.mean(x32 * x32, axis=-1, keepdims=True) + EPS)
    return (y * g.astype(jnp.float32)).astype(x.dtype)


def modulate(h, shift, scale):
    return h * (1.0 + scale[:, None, :]) + shift[:, None, :]


def swiglu(h, w_gate, w_up, w_down):
    return (jax.nn.silu(h @ w_gate) * (h @ w_up)) @ w_down


def dilated_band_attention(q, k, v, window, dilation):
    B, S, H, E = q.shape
    nw = window // dilation
    L = S // dilation
    nb = -(-L // nw)
    Lp = nb * nw

    def to_blocks(t):
        t = t.reshape(B, L, dilation, H, E).transpose(0, 2, 3, 1, 4)
        t = jnp.pad(t, ((0, 0), (0, 0), (0, 0), (0, Lp - L), (0, 0)))
        return t.reshape(B, dilation, H, nb, nw, E)

    def with_prev(t):
        prev = jnp.pad(t[:, :, :, :-1], ((0, 0), (0, 0), (0, 0), (1, 0), (0, 0), (0, 0)))
        return jnp.concatenate([prev, t], axis=4)

    qb = to_blocks(q)
    kc = with_prev(to_blocks(k))
    vc = with_prev(to_blocks(v))
    s = jnp.einsum('brhnqe,brhnke->brhnqk', qb, kc) * (E ** -0.5)
    qi = jnp.arange(nw)
    kj = jnp.arange(2 * nw)
    rel = nw + qi[:, None] - kj[None, :]
    band = (rel >= 0) & (rel <= nw)
    key_pos = jnp.arange(nb)[:, None] * nw - nw + kj[None, :]
    mask = band[None, :, :] & (key_pos >= 0)[:, None, :]
    s = jnp.where(mask, s, -jnp.inf)
    m = jnp.max(s, axis=-1, keepdims=True)
    p = jnp.exp(s - m)
    denom = jnp.sum(p, axis=-1, keepdims=True)
    o = jnp.einsum('brhnqk,brhnke->brhnqe', p, vc) / denom
    lse = (m + jnp.log(denom))[..., 0]
    o = o.reshape(B, dilation, H, Lp, E)[:, :, :, :L].transpose(0, 3, 1, 2, 4).reshape(B, S, H, E)
    lse = lse.reshape(B, dilation, H, Lp)[:, :, :, :L].transpose(0, 3, 1, 2).reshape(B, S, H)
    return o, lse


def hybrid_mixer(h, w_in, q_norm, k_norm, conv_w, w_attn_branch, w_conv_branch, w_out):
    B, S, _ = h.shape
    proj = h @ w_in
    q, k, v, u, b_gate, c_gate, g_attn, g_conv = jnp.split(proj, IN_SPLITS, axis=-1)
    q = rms_norm(q.reshape(B, S, N_GROUPS, HEADS_PER_GROUP, HEAD_DIM), q_norm).astype(jnp.float32)
    k = rms_norm(k.reshape(B, S, N_GROUPS, HEADS_PER_GROUP, HEAD_DIM), k_norm).astype(jnp.float32)
    v = v.reshape(B, S, N_GROUPS, HEADS_PER_GROUP, HEAD_DIM).astype(jnp.float32)
    outs, lses = [], []
    for g, (window, dilation) in enumerate(DILATED_CONFIGS):
        o_g, lse_g = dilated_band_attention(q[:, :, g], k[:, :, g], v[:, :, g], window, dilation)
        outs.append(o_g)
        lses.append(lse_g)
    weights = jax.nn.softmax(jnp.stack(lses, axis=0), axis=0)
    o = jnp.einsum('gbsh,gbshe->bshe', weights, jnp.stack(outs, axis=0))
    y_attn = o.reshape(B, S, ATTN_OUT).astype(h.dtype) @ w_attn_branch
    xc = c_gate * u
    xp = jnp.pad(xc, ((0, 0), (CONV_K - 1, 0), (0, 0)))
    conv = xp[:, 0:S] * conv_w[0]
    for j in range(1, CONV_K):
        conv = conv + xp[:, j:j + S] * conv_w[j]
    y_conv = (b_gate * conv) @ w_conv_branch
    merged = jax.nn.sigmoid(g_attn) * y_attn + jax.nn.sigmoid(g_conv) * y_conv
    return merged @ w_out


def setup_inputs(seed: int = 0) -> dict:
    key = jax.random.key(seed)
    ks = jax.random.split(key, 24)
    f32 = jnp.float32
    L, D = DEPTH, D_MODEL

    def nrm(k, shape, scale):
        return jax.random.normal(k, shape, f32) * scale

    return {
        'x': nrm(ks[0], (BATCH, SEQ, D), 1.0),
        'c': nrm(ks[1], (BATCH, D), 1.0),
        'w_ada': nrm(ks[2], (L, D, N_MOD * D), 0.5 * D ** -0.5),
        'b_ada': nrm(ks[3], (L, N_MOD * D), 0.01),
        'norm_ffn1': 1.0 + nrm(ks[4], (L, D), 0.01),
        'ffn1_w_gate': nrm(ks[5], (L, D, D_FF), D ** -0.5),
        'ffn1_w_up': nrm(ks[6], (L, D, D_FF), D ** -0.5),
        'ffn1_w_down': nrm(ks[7], (L, D_FF, D), D_FF ** -0.5),
        'norm_mix': 1.0 + nrm(ks[8], (L, D), 0.01),
        'w_in': nrm(ks[9], (L, D, IN_WIDTH), D ** -0.5),
        'q_norm': 1.0 + nrm(ks[10], (L, HEAD_DIM), 0.01),
        'k_norm': 1.0 + nrm(ks[11], (L, HEAD_DIM), 0.01),
        'conv_w': nrm(ks[12], (L, CONV_K, CONV_WIDTH), CONV_K ** -0.5),
        'w_attn_branch': nrm(ks[13], (L, ATTN_OUT, D), ATTN_OUT ** -0.5),
        'w_conv_branch': nrm(ks[14], (L, CONV_WIDTH, D), CONV_WIDTH ** -0.5),
        'w_out': nrm(ks[15], (L, D, D), D ** -0.5),
        'norm_ffn2': 1.0 + nrm(ks[16], (L, D), 0.01),
        'ffn2_w_gate': nrm(ks[17], (L, D, D_FF), D ** -0.5),
        'ffn2_w_up': nrm(ks[18], (L, D, D_FF), D ** -0.5),
        'ffn2_w_down': nrm(ks[19], (L, D_FF, D), D_FF ** -0.5),
    }


def reference(x, c, w_ada, b_ada, norm_ffn1, ffn1_w_gate, ffn1_w_up, ffn1_w_down,
              norm_mix, w_in, q_norm, k_norm, conv_w, w_attn_branch, w_conv_branch, w_out,
              norm_ffn2, ffn2_w_gate, ffn2_w_up, ffn2_w_down):
    c_act = jax.nn.silu(c)
    for l in range(DEPTH):
        mod = c_act @ w_ada[l] + b_ada[l]
        sh1, sc1, gt1, sh2, sc2, gt2, sh3, sc3, gt3 = jnp.split(mod, N_MOD, axis=-1)
        h = modulate(rms_norm(x, norm_ffn1[l]), sh1, sc1)
        x = x + 0.5 * gt1[:, None, :] * swiglu(h, ffn1_w_gate[l], ffn1_w_up[l], ffn1_w_down[l])
        h = modulate(rms_norm(x, norm_mix[l]), sh2, sc2)
        x = x + gt2[:, None, :] * hybrid_mixer(h, w_in[l], q_norm[l], k_norm[l], conv_w[l],
                                               w_attn_branch[l], w_conv_branch[l], w_out[l])
        h = modulate(rms_norm(x, norm_ffn2[l]), sh3, sc3)
        x = x + 0.5 * gt3[:, None, :] * swiglu(h, ffn2_w_gate[l], ffn2_w_up[l], ffn2_w_down[l])
    return x
```

```python
import functools
import math

import jax
import jax.numpy as jnp
from jax import lax
from jax.experimental import pallas as pl
from jax.experimental.pallas import tpu as pltpu

EPS = 1e-6
N_MOD = 9
N_GROUPS = 3
HEADS = 4
HEAD_DIM = 128
DILATIONS = (1, 4, 16)
BAND = 128
CHUNK = BAND * max(DILATIONS)
CONV_K = 3

V7X_VMEM_BYTES = 64 * 1024 * 1024
VMEM_LIMIT = V7X_VMEM_BYTES - 8 * 1024 * 1024

BF16 = jnp.bfloat16
F32 = jnp.float32


def _resident(shape):
    return pl.BlockSpec(shape, lambda *_: (0,) * len(shape), pipeline_mode=pl.Buffered(1))


def _params(n_axes):
    return pltpu.CompilerParams(dimension_semantics=("arbitrary",) * n_axes, vmem_limit_bytes=VMEM_LIMIT)


def _rms_mod(x, gain, shift, scale):
    ms = jnp.mean(x * x, axis=-1, keepdims=True)
    y = x * lax.rsqrt(ms + EPS) * gain
    return y * (1.0 + scale) + shift


def _dot(a, b):
    return jnp.dot(a, b, preferred_element_type=F32)


def _adaln_kernel(c_ref, w_ref, b_ref, o_ref):
    c = c_ref[...]
    c_act = c * jax.nn.sigmoid(c)
    o_ref[...] = _dot(c_act, w_ref[...]) + b_ref[...]


def _adaln(c_pad, w_ada, b_ada, tn=1152):
    rows, d = c_pad.shape
    n = w_ada.shape[1]
    return pl.pallas_call(
        _adaln_kernel,
        out_shape=jax.ShapeDtypeStruct((rows, n), F32),
        grid=(n // tn,),
        in_specs=[
            pl.BlockSpec((rows, d), lambda j: (0, 0)),
            pl.BlockSpec((d, tn), lambda j: (0, j)),
            pl.BlockSpec((1, tn), lambda j: (0, j)),
        ],
        out_specs=pl.BlockSpec((rows, tn), lambda j: (0, j)),
        compiler_params=_params(1),
        name="adaln",
    )(c_pad, w_ada, b_ada)


def _ffn_kernel(mod_row, x_ref, mod_ref, g_ref, wg_ref, wu_ref, wd_ref, o_ref):
    x = x_ref[...]
    shift = mod_ref[mod_row:mod_row + 1, :]
    scale = mod_ref[mod_row + 1:mod_row + 2, :]
    gate = mod_ref[mod_row + 2:mod_row + 3, :]
    h = _rms_mod(x, g_ref[...], shift, scale).astype(BF16)
    a = _dot(h, wg_ref[...])
    u = _dot(h, wu_ref[...])
    act = (a * jax.nn.sigmoid(a) * u).astype(BF16)
    y = _dot(act, wd_ref[...])
    o_ref[...] = x + 0.5 * gate * y


def _ffn(x, mod, gain, w_gate, w_up, w_down, mod_row, tm=512):
    b, s, d = x.shape
    f = w_gate.shape[1]
    return pl.pallas_call(
        functools.partial(_ffn_kernel, mod_row),
        out_shape=jax.ShapeDtypeStruct(x.shape, F32),
        grid=(b, s // tm),
        in_specs=[
            pl.BlockSpec((None, tm, d), lambda bi, i: (bi, i, 0)),
            pl.BlockSpec((None, N_MOD, d), lambda bi, i: (bi, 0, 0)),
            _resident((1, d)),
            _resident((d, f)),
            _resident((d, f)),
            _resident((f, d)),
        ],
        out_specs=pl.BlockSpec((None, tm, d), lambda bi, i: (bi, i, 0)),
        compiler_params=_params(2),
        name="ffn",
    )(x, mod, gain, w_gate, w_up, w_down)


def _head_rms(t, gain, post_scale):
    outs = []
    for hd in range(HEADS):
        th = t[:, hd * HEAD_DIM:(hd + 1) * HEAD_DIM]
        ms = jnp.mean(th * th, axis=-1, keepdims=True)
        outs.append(th * lax.rsqrt(ms + EPS) * (gain * post_scale))
    return jnp.concatenate(outs, axis=-1)


def _proj_kernel(n_slabs, with_conv, x_ref, mod_ref, g_ref, qn_ref, kn_ref, w_ref, *rest):
    if with_conv:
        q_ref, k_ref, v_ref, xc_ref, bg_ref, sga_ref, sgc_ref, h_scr = rest
    else:
        q_ref, k_ref, v_ref, h_scr = rest
    d = g_ref.shape[-1]
    tm = h_scr.shape[0]
    rows = tm // n_slabs
    shift = mod_ref[3:4, :]
    scale = mod_ref[4:5, :]
    for sl in range(n_slabs):
        xs = x_ref[:, sl * d:(sl + 1) * d]
        h_scr[sl * rows:(sl + 1) * rows, :] = _rms_mod(xs, g_ref[...], shift, scale).astype(BF16)
    h = h_scr[...]
    aw = HEADS * HEAD_DIM
    q = _dot(h, w_ref[:, 0:aw])
    q_ref[...] = _head_rms(q, qn_ref[...], HEAD_DIM ** -0.5).astype(BF16)
    k = _dot(h, w_ref[:, aw:2 * aw])
    k_ref[...] = _head_rms(k, kn_ref[...], 1.0).astype(BF16)
    v_ref[...] = _dot(h, w_ref[:, 2 * aw:3 * aw]).astype(BF16)
    if with_conv:
        o = 3 * aw
        u = _dot(h, w_ref[:, o:o + d])
        cg = _dot(h, w_ref[:, o + 2 * d:o + 3 * d])
        xc_ref[...] = cg * u
        bg_ref[...] = _dot(h, w_ref[:, o + d:o + 2 * d])
        sga_ref[...] = jax.nn.sigmoid(_dot(h, w_ref[:, o + 3 * d:o + 4 * d]))
        sgc_ref[...] = jax.nn.sigmoid(_dot(h, w_ref[:, o + 4 * d:o + 5 * d]))


def _proj(x, mod, gain, q_norm, k_norm, w, dilation, with_conv, tm=512):
    b, s, d = x.shape
    aw = HEADS * HEAD_DIM
    n_chunks = s // CHUNK
    tiles_per_chunk = CHUNK // tm
    if dilation == 1:
        n_slabs = 1
        xv = x
        x_spec = pl.BlockSpec((None, tm, d), lambda bi, c, t: (bi, c * tiles_per_chunk + t, 0))
    else:
        xv = x.reshape(b, s // dilation, dilation * d)
        res_per_tile = dilation // tiles_per_chunk
        n_slabs = max(res_per_tile, 1)
        if res_per_tile >= 1:
            x_spec = pl.BlockSpec((None, CHUNK // dilation, n_slabs * d), lambda bi, c, t: (bi, c, t))
        else:
            sub = tiles_per_chunk // dilation
            x_spec = pl.BlockSpec((None, tm, d),
                                  lambda bi, c, t: (bi, c * sub + t % sub, t // sub))
    row_spec = lambda width: pl.BlockSpec((None, tm, width), lambda bi, c, t: (bi, c * tiles_per_chunk + t, 0))
    out_shape = [jax.ShapeDtypeStruct((b, s, aw), BF16)] * 3
    out_specs = [row_spec(aw)] * 3
    if with_conv:
        out_shape += [jax.ShapeDtypeStruct((b, s, d), F32)] * 4
        out_specs += [row_spec(d)] * 4
    return pl.pallas_call(
        functools.partial(_proj_kernel, n_slabs, with_conv),
        out_shape=out_shape,
        grid=(b, n_chunks, tiles_per_chunk),
        in_specs=[
            x_spec,
            pl.BlockSpec((None, N_MOD, d), lambda bi, c, t: (bi, 0, 0)),
            _resident((1, d)),
            _resident((1, HEAD_DIM)),
            _resident((1, HEAD_DIM)),
            _resident(w.shape),
        ],
        out_specs=out_specs,
        scratch_shapes=[pltpu.VMEM((tm, d), BF16)],
        compiler_params=_params(3),
        name=f"proj_d{dilation}",
    )(xv, mod, gain, q_norm, k_norm, w)


def _attn_block(q, kcat, vcat, bias):
    s = lax.dot_general(q, kcat, (((1,), (1,)), ((), ())), preferred_element_type=F32) + bias
    m = jnp.max(s, axis=-1, keepdims=True)
    p = jnp.exp(s - m)
    den = jnp.sum(p, axis=-1, keepdims=True)
    o = _dot(p.astype(BF16), vcat) / den
    return o, m + jnp.log(den)


def _attn_kernel(*refs):
    ins, outs = refs[:5 * N_GROUPS], refs[5 * N_GROUPS:]
    c = pl.program_id(1)
    qi = lax.broadcasted_iota(jnp.int32, (BAND, 2 * BAND), 0)
    kj = lax.broadcasted_iota(jnp.int32, (BAND, 2 * BAND), 1)
    in_prev = (kj < BAND) & (kj >= qi)
    in_cur = (kj >= BAND) & (kj - BAND <= qi)
    neg = jnp.float32(-jnp.inf)
    bias_in = jnp.where(in_prev | in_cur, 0.0, neg)
    bias_first = jnp.where(in_cur, 0.0, neg)
    bias_edge = jnp.where(c > 0, bias_in, bias_first)
    for g, dil in enumerate(DILATIONS):
        q_ref, kc_ref, kp_ref, vc_ref, vp_ref = ins[5 * g:5 * g + 5]
        o_ref, l_ref = outs[2 * g:2 * g + 2]
        sub_len = CHUNK // dil
        blocks = sub_len // BAND
        for r in range(dil):
            for jb in range(blocks):
                row = r * sub_len + jb * BAND
                q = q_ref[row:row + BAND, :]
                if jb == 0:
                    prev = r * sub_len + sub_len - BAND
                    kcat = jnp.concatenate([kp_ref[prev:prev + BAND, :], kc_ref[row:row + BAND, :]], axis=0)
                    vcat = jnp.concatenate([vp_ref[prev:prev + BAND, :], vc_ref[row:row + BAND, :]], axis=0)
                    bias = bias_edge
                else:
                    kcat = kc_ref[row - BAND:row + BAND, :]
                    vcat = vc_ref[row - BAND:row + BAND, :]
                    bias = bias_in
                o, lse = _attn_block(q, kcat, vcat, bias)
                o_ref[jb * BAND:(jb + 1) * BAND, r * HEAD_DIM:(r + 1) * HEAD_DIM] = o
                l_ref[jb * BAND:(jb + 1) * BAND, r * HEAD_DIM:(r + 1) * HEAD_DIM] = jnp.broadcast_to(
                    lse, (BAND, HEAD_DIM))


def _attn(qkv):
    b, s, _ = qkv[0][0].shape
    n_chunks = s // CHUNK
    cur = pl.BlockSpec((None, CHUNK, HEAD_DIM), lambda bi, c, h: (bi, c, h))
    prv = pl.BlockSpec((None, CHUNK, HEAD_DIM), lambda bi, c, h: (bi, jnp.maximum(c - 1, 0), h))
    args, in_specs, out_shape, out_specs = [], [], [], []
    for (q, k, v), dil in zip(qkv, DILATIONS):
        args += [q, k, k, v, v]
        in_specs += [cur, cur, prv, cur, prv]
        view = jax.ShapeDtypeStruct((HEADS, b, s // dil, dil * HEAD_DIM), F32)
        spec = pl.BlockSpec((None, None, CHUNK // dil, dil * HEAD_DIM), lambda bi, c, h: (h, bi, c, 0))
        out_shape += [view, view]
        out_specs += [spec, spec]
    outs = pl.pallas_call(
        _attn_kernel,
        out_shape=out_shape,
        grid=(b, n_chunks, HEADS),
        in_specs=in_specs,
        out_specs=out_specs,
        compiler_params=_params(3),
        name="attn",
    )(*args)
    outs = [t.reshape(HEADS, b, s, HEAD_DIM) for t in outs]
    return [(outs[2 * g], outs[2 * g + 1]) for g in range(N_GROUPS)]


def _mixout_kernel(x_ref, mod_ref, o0, l0, o1, l1, o2, l2, xc_ref, halo_ref, bg_ref, sga_ref, sgc_ref,
                   cw_ref, wa_ref, wc_ref, wo_ref, out_ref):
    tm = x_ref.shape[0]
    i = pl.program_id(1)
    heads = []
    for hd in range(HEADS):
        ls = [l0[hd], l1[hd], l2[hd]]
        os_ = [o0[hd], o1[hd], o2[hd]]
        m = jnp.maximum(jnp.maximum(ls[0], ls[1]), ls[2])
        es = [jnp.exp(l - m) for l in ls]
        num = es[0] * os_[0] + es[1] * os_[1] + es[2] * os_[2]
        heads.append(num / (es[0] + es[1] + es[2]))
    o = jnp.concatenate(heads, axis=-1).astype(BF16)
    y_attn = _dot(o, wa_ref[...])
    xc = xc_ref[...]
    halo = jnp.where(i > 0, halo_ref[...], 0.0)
    ext = jnp.concatenate([halo, xc], axis=0)
    pad = halo.shape[0]
    conv = xc * cw_ref[CONV_K - 1:CONV_K, :]
    for j in range(CONV_K - 1):
        back = CONV_K - 1 - j
        conv = conv + ext[pad - back:pad - back + tm, :] * cw_ref[j:j + 1, :]
    y_conv = _dot((bg_ref[...] * conv).astype(BF16), wc_ref[...])
    merged = (sga_ref[...] * y_attn + sgc_ref[...] * y_conv).astype(BF16)
    out_ref[...] = x_ref[...] + mod_ref[5:6, :] * _dot(merged, wo_ref[...])


def _mixout(x, mod, attn_out, xc, bg, sga, sgc, conv_w, w_attn, w_conv, w_out, tm=512):
    b, s, d = x.shape
    halo_rows = 8
    row = pl.BlockSpec((None, tm, d), lambda bi, i: (bi, i, 0))
    head = pl.BlockSpec((HEADS, None, tm, HEAD_DIM), lambda bi, i: (0, bi, i, 0))
    halo = pl.BlockSpec((None, halo_rows, d),
                        lambda bi, i: (bi, jnp.maximum(i * (tm // halo_rows) - 1, 0), 0))
    attn_args, attn_specs = [], []
    for o, l in attn_out:
        attn_args += [o, l]
        attn_specs += [head, head]
    return pl.pallas_call(
        _mixout_kernel,
        out_shape=jax.ShapeDtypeStruct(x.shape, F32),
        grid=(b, s // tm),
        in_specs=[row, pl.BlockSpec((None, N_MOD, d), lambda bi, i: (bi, 0, 0))] + attn_specs + [
            row, halo, row, row, row,
            _resident(conv_w.shape), _resident(w_attn.shape), _resident(w_conv.shape), _resident(w_out.shape),
        ],
        out_specs=row,
        compiler_params=_params(2),
        name="mixout",
    )(x, mod, *attn_args, xc, xc, bg, sga, sgc, conv_w, w_attn, w_conv, w_out)


def kernel(x, c, w_ada, b_ada, norm_ffn1, ffn1_w_gate, ffn1_w_up, ffn1_w_down, norm_mix, w_in, q_norm, k_norm,
           conv_w, w_attn_branch, w_conv_branch, w_out, norm_ffn2, ffn2_w_gate, ffn2_w_up, ffn2_w_down):
    b, s, d = x.shape
    depth = w_ada.shape[0]
    aw = HEADS * HEAD_DIM
    qkv_w = N_GROUPS * aw
    c_pad = jnp.zeros((8, d), F32).at[:b].set(c)
    for l in range(depth):
        mod = _adaln(c_pad, w_ada[l], b_ada[l][None, :])[:b].reshape(b, N_MOD, d)
        x = _ffn(x, mod, norm_ffn1[l][None, :], ffn1_w_gate[l].astype(BF16), ffn1_w_up[l].astype(BF16),
                 ffn1_w_down[l].astype(BF16), mod_row=0)
        w = w_in[l].astype(BF16)
        qn, kn = q_norm[l][None, :], k_norm[l][None, :]
        qkv = []
        conv_in = None
        for g, dil in enumerate(DILATIONS):
            cols = [w[:, j * qkv_w + g * aw:j * qkv_w + (g + 1) * aw] for j in range(3)]
            if g == 0:
                cols.append(w[:, 3 * qkv_w:])
            res = _proj(x, mod, norm_mix[l][None, :], qn, kn, jnp.concatenate(cols, axis=1), dil, g == 0)
            qkv.append(tuple(res[:3]))
            if g == 0:
                conv_in = res[3:]
        attn_out = _attn(qkv)
        x = _mixout(x, mod, attn_out, *conv_in, conv_w[l], w_attn_branch[l].astype(BF16),
                    w_conv_branch[l].astype(BF16), w_out[l].astype(BF16))
        x = _ffn(x, mod, norm_ffn2[l][None, :], ffn2_w_gate[l].astype(BF16), ffn2_w_up[l].astype(BF16),
                 ffn2_w_down[l].astype(BF16), mod_row=6)
    return x
```

```python
import functools

import jax
import jax.numpy as jnp
from jax import lax
from jax.experimental import pallas as pl
from jax.experimental.pallas import tpu as pltpu

EPS = 1e-6
N_MOD = 9
N_GROUPS = 3
HEADS = 4
HEAD_DIM = 128
DILATIONS = (1, 4, 16)
BAND = 128
CHUNK = BAND * max(DILATIONS)
TILE = 512
CONV_K = 3

V7X_VMEM_BYTES = 64 * 1024 * 1024
VMEM_LIMIT = V7X_VMEM_BYTES - 8 * 1024 * 1024

BF16 = jnp.bfloat16
F32 = jnp.float32


def _resident(shape):
    return pl.BlockSpec(shape, lambda *_: (0,) * len(shape), pipeline_mode=pl.Buffered(1))


def _params(n_axes):
    return pltpu.CompilerParams(dimension_semantics=("arbitrary",) * n_axes, vmem_limit_bytes=VMEM_LIMIT)


def _rms_mod(x, gain, shift, scale):
    ms = jnp.mean(x * x, axis=-1, keepdims=True)
    y = x * lax.rsqrt(ms + EPS) * gain
    return y * (1.0 + scale) + shift


def _dot(a, b):
    return jnp.dot(a, b, preferred_element_type=F32)


def _adaln_kernel(c_ref, w_ref, b_ref, o_ref):
    c = c_ref[...]
    c_act = c * jax.nn.sigmoid(c)
    o_ref[...] = _dot(c_act, w_ref[...]) + b_ref[...]


def _adaln(c_pad, w_ada, b_ada, tn=1152):
    rows, d = c_pad.shape
    n = w_ada.shape[1]
    return pl.pallas_call(
        _adaln_kernel,
        out_shape=jax.ShapeDtypeStruct((rows, n), F32),
        grid=(n // tn,),
        in_specs=[
            pl.BlockSpec((rows, d), lambda j: (0, 0)),
            pl.BlockSpec((d, tn), lambda j: (0, j)),
            pl.BlockSpec((1, tn), lambda j: (0, j)),
        ],
        out_specs=pl.BlockSpec((rows, tn), lambda j: (0, j)),
        compiler_params=_params(1),
        name="adaln",
    )(c_pad, w_ada, b_ada)


def _ffn_kernel(mod_row, x_ref, mod_ref, g_ref, wg_ref, wu_ref, wd_ref, o_ref):
    x = x_ref[...]
    shift = mod_ref[mod_row:mod_row + 1, :]
    scale = mod_ref[mod_row + 1:mod_row + 2, :]
    gate = mod_ref[mod_row + 2:mod_row + 3, :]
    h = _rms_mod(x, g_ref[...], shift, scale).astype(BF16)
    a = _dot(h, wg_ref[...])
    u = _dot(h, wu_ref[...])
    act = (a * jax.nn.sigmoid(a) * u).astype(BF16)
    y = _dot(act, wd_ref[...])
    o_ref[...] = x + 0.5 * gate * y


def _ffn(x, mod, gain, w_gate, w_up, w_down, mod_row, tm=512):
    b, s, d = x.shape
    f = w_gate.shape[1]
    return pl.pallas_call(
        functools.partial(_ffn_kernel, mod_row),
        out_shape=jax.ShapeDtypeStruct(x.shape, F32),
        grid=(b, s // tm),
        in_specs=[
            pl.BlockSpec((None, tm, d), lambda bi, i: (bi, i, 0)),
            pl.BlockSpec((None, N_MOD, d), lambda bi, i: (bi, 0, 0)),
            _resident((1, d)),
            _resident((d, f)),
            _resident((d, f)),
            _resident((f, d)),
        ],
        out_specs=pl.BlockSpec((None, tm, d), lambda bi, i: (bi, i, 0)),
        compiler_params=_params(2),
        name="ffn",
    )(x, mod, gain, w_gate, w_up, w_down)


def _store_by_residue(slab, levels, out_ref, col, scr_a, scr_b):
    scr_a[...] = slab
    q = TILE // 4
    if levels == 1:
        for r in range(4):
            out_ref[r * q:(r + 1) * q, col:col + HEAD_DIM] = scr_a[pl.ds(r, q, stride=4), :].astype(BF16)
        return
    for r in range(4):
        scr_b[r * q:(r + 1) * q, :] = scr_a[pl.ds(r, q, stride=4), :]
    qq = q // 4
    for r in range(4):
        for r2 in range(4):
            p = r * 4 + r2
            out_ref[p * qq:(p + 1) * qq, col:col + HEAD_DIM] = scr_b[pl.ds(r * q + r2, qq, stride=4), :].astype(BF16)


def _proj_kernel(x_ref, mod_ref, g_ref, qn_ref, kn_ref, w_ref, *rest):
    qkv_refs = rest[:3 * N_GROUPS]
    xc_ref, bg_ref, sga_ref, sgc_ref, scr_a, scr_b = rest[3 * N_GROUPS:]
    d = g_ref.shape[-1]
    aw = HEADS * HEAD_DIM
    h = _rms_mod(x_ref[...], g_ref[...], mod_ref[3:4, :], mod_ref[4:5, :]).astype(BF16)
    gains = (qn_ref[...] * HEAD_DIM ** -0.5, kn_ref[...], None)
    for j in range(3):
        for g in range(N_GROUPS):
            c0 = (j * N_GROUPS + g) * aw
            t = _dot(h, w_ref[:, c0:c0 + aw])
            out_ref = qkv_refs[3 * g + j]
            for hd in range(HEADS):
                col = hd * HEAD_DIM
                slab = t[:, col:col + HEAD_DIM]
                if gains[j] is not None:
                    ms = jnp.mean(slab * slab, axis=-1, keepdims=True)
                    slab = slab * lax.rsqrt(ms + EPS) * gains[j]
                if g == 0:
                    out_ref[:, col:col + HEAD_DIM] = slab.astype(BF16)
                else:
                    _store_by_residue(slab, g, out_ref, col, scr_a.at[hd], scr_b.at[hd])
    o = 3 * N_GROUPS * aw
    u = _dot(h, w_ref[:, o:o + d])
    cg = _dot(h, w_ref[:, o + 2 * d:o + 3 * d])
    xc_ref[...] = (cg * u).astype(BF16)
    bg_ref[...] = _dot(h, w_ref[:, o + d:o + 2 * d]).astype(BF16)
    sga_ref[...] = jax.nn.sigmoid(_dot(h, w_ref[:, o + 3 * d:o + 4 * d])).astype(BF16)
    sgc_ref[...] = jax.nn.sigmoid(_dot(h, w_ref[:, o + 4 * d:o + 5 * d])).astype(BF16)


def _proj(x, mod, gain, q_norm, k_norm, w):
    b, s, d = x.shape
    aw = HEADS * HEAD_DIM
    row = lambda width: pl.BlockSpec((None, TILE, width), lambda bi, i: (bi, i, 0))
    return pl.pallas_call(
        _proj_kernel,
        out_shape=[jax.ShapeDtypeStruct((b, s, aw), BF16)] * (3 * N_GROUPS) + [jax.ShapeDtypeStruct((b, s, d), BF16)] * 4,
        grid=(b, s // TILE),
        in_specs=[
            row(d),
            pl.BlockSpec((None, N_MOD, d), lambda bi, i: (bi, 0, 0)),
            _resident((1, d)),
            _resident((1, HEAD_DIM)),
            _resident((1, HEAD_DIM)),
            _resident(w.shape),
        ],
        out_specs=[row(aw)] * (3 * N_GROUPS) + [row(d)] * 4,
        scratch_shapes=[pltpu.VMEM((HEADS, TILE, HEAD_DIM), F32), pltpu.VMEM((HEADS, TILE, HEAD_DIM), F32)],
        compiler_params=_params(2),
        name="proj",
    )(x, mod, gain, q_norm, k_norm, w)


def _attn_block(q, kcat, vcat, bias):
    s = lax.dot_general(q, kcat, (((1,), (1,)), ((), ())), preferred_element_type=F32) + bias
    m = jnp.max(s, axis=-1, keepdims=True)
    p = jnp.exp(s - m)
    den = jnp.sum(p, axis=-1, keepdims=True)
    o = _dot(p.astype(BF16), vcat) / den
    return o, m + jnp.log(den)


def _band_rows(dil, r, jb):
    per_tile = TILE // dil
    if dil == 1:
        return [(jb * BAND, BAND)]
    pos = r if dil == 4 else (r % 4) * 4 + r // 4
    tiles = BAND // per_tile
    return [((jb * tiles + t) * TILE + pos * per_tile, per_tile) for t in range(tiles)]


def _gather(ref, ranges):
    parts = [ref[a:a + n, :] for a, n in ranges]
    return parts[0] if len(parts) == 1 else jnp.concatenate(parts, axis=0)


def _attn_kernel(*refs):
    ins = refs[:3 * N_GROUPS]
    out_ref = refs[3 * N_GROUPS]
    scr = refs[3 * N_GROUPS + 1:]
    kprev, vprev = scr[0:N_GROUPS], scr[N_GROUPS:2 * N_GROUPS]
    o_nat, l_nat = scr[2 * N_GROUPS:3 * N_GROUPS], scr[3 * N_GROUPS:4 * N_GROUPS]
    c = pl.program_id(2)

    @pl.when(c == 0)
    def _():
        for ref in kprev + vprev:
            ref[...] = jnp.zeros(ref.shape, ref.dtype)

    qi = lax.broadcasted_iota(jnp.int32, (BAND, 2 * BAND), 0)
    kj = lax.broadcasted_iota(jnp.int32, (BAND, 2 * BAND), 1)
    in_prev = (kj < BAND) & (kj >= qi)
    in_cur = (kj >= BAND) & (kj - BAND <= qi)
    neg = jnp.float32(-jnp.inf)
    bias_in = jnp.where(in_prev | in_cur, 0.0, neg)
    bias_first = jnp.where(in_cur, 0.0, neg)
    bias_edge = jnp.where(c > 0, bias_in, bias_first)

    for g, dil in enumerate(DILATIONS):
        q_ref, k_ref, v_ref = ins[3 * g:3 * g + 3]
        blocks = CHUNK // dil // BAND
        carry = [rg for r in range(dil) for rg in _band_rows(dil, r, blocks - 1)]
        for r in range(dil):
            for jb in range(blocks):
                cur = _band_rows(dil, r, jb)
                q = _gather(q_ref, cur)
                if jb == 0:
                    base = r * BAND
                    k_prev = kprev[g][base:base + BAND, :]
                    v_prev = vprev[g][base:base + BAND, :]
                    bias = bias_edge
                else:
                    prev = _band_rows(dil, r, jb - 1)
                    k_prev, v_prev = _gather(k_ref, prev), _gather(v_ref, prev)
                    bias = bias_in
                kcat = jnp.concatenate([k_prev, _gather(k_ref, cur)], axis=0)
                vcat = jnp.concatenate([v_prev, _gather(v_ref, cur)], axis=0)
                o, lse = _attn_block(q, kcat, vcat, bias)
                if dil == 1:
                    rows = pl.ds(jb * BAND, BAND)
                else:
                    rows = pl.ds(jb * BAND * dil + r, BAND, stride=dil)
                o_nat[g][rows, :] = o
                l_nat[g][rows, :] = jnp.broadcast_to(lse, (BAND, HEAD_DIM))
        off = 0
        for a, n in carry:
            kprev[g][off:off + n, :] = k_ref[a:a + n, :]
            vprev[g][off:off + n, :] = v_ref[a:a + n, :]
            off += n

    step = 256
    for a in range(0, CHUNK, step):
        ls = [l_nat[g][a:a + step, :] for g in range(N_GROUPS)]
        m = jnp.maximum(jnp.maximum(ls[0], ls[1]), ls[2])
        es = [jnp.exp(l - m) for l in ls]
        num = es[0] * o_nat[0][a:a + step, :] + es[1] * o_nat[1][a:a + step, :] + es[2] * o_nat[2][a:a + step, :]
        out_ref[a:a + step, :] = (num / (es[0] + es[1] + es[2])).astype(BF16)


def _attn(qkv):
    b, s, aw = qkv[0].shape
    blk = pl.BlockSpec((None, CHUNK, HEAD_DIM), lambda bi, h, c: (bi, c, h))
    carry_rows = [BAND * dil for dil in DILATIONS]
    return pl.pallas_call(
        _attn_kernel,
        out_shape=jax.ShapeDtypeStruct((b, s, aw), BF16),
        grid=(b, HEADS, s // CHUNK),
        in_specs=[blk] * len(qkv),
        out_specs=blk,
        scratch_shapes=(
            [pltpu.VMEM((n, HEAD_DIM), BF16) for n in carry_rows] * 2
            + [pltpu.VMEM((CHUNK, HEAD_DIM), F32)] * (2 * N_GROUPS)),
        compiler_params=_params(3),
        name="attn",
    )(*qkv)


def _mixout_kernel(x_ref, mod_ref, o_ref, xc_ref, halo_ref, bg_ref, sga_ref, sgc_ref,
                   cw_ref, wa_ref, wc_ref, wo_ref, out_ref):
    tm = x_ref.shape[0]
    i = pl.program_id(1)
    y_attn = _dot(o_ref[...], wa_ref[...])
    xc = xc_ref[...].astype(F32)
    halo = jnp.where(i > 0, halo_ref[...].astype(F32), 0.0)
    ext = jnp.concatenate([halo, xc], axis=0)
    pad = halo.shape[0]
    conv = xc * cw_ref[CONV_K - 1:CONV_K, :]
    for j in range(CONV_K - 1):
        back = CONV_K - 1 - j
        conv = conv + ext[pad - back:pad - back + tm, :] * cw_ref[j:j + 1, :]
    y_conv = _dot((bg_ref[...].astype(F32) * conv).astype(BF16), wc_ref[...])
    merged = (sga_ref[...].astype(F32) * y_attn + sgc_ref[...].astype(F32) * y_conv).astype(BF16)
    out_ref[...] = x_ref[...] + mod_ref[5:6, :] * _dot(merged, wo_ref[...])


def _mixout(x, mod, o, xc, bg, sga, sgc, conv_w, w_attn, w_conv, w_out, tm=512):
    b, s, d = x.shape
    halo_rows = 16
    row = lambda width: pl.BlockSpec((None, tm, width), lambda bi, i: (bi, i, 0))
    halo = pl.BlockSpec((None, halo_rows, d),
                        lambda bi, i: (bi, jnp.maximum(i * (tm // halo_rows) - 1, 0), 0))
    return pl.pallas_call(
        _mixout_kernel,
        out_shape=jax.ShapeDtypeStruct(x.shape, F32),
        grid=(b, s // tm),
        in_specs=[row(d), pl.BlockSpec((None, N_MOD, d), lambda bi, i: (bi, 0, 0)), row(o.shape[-1]),
                  row(d), halo, row(d), row(d), row(d),
                  _resident(conv_w.shape), _resident(w_attn.shape), _resident(w_conv.shape), _resident(w_out.shape)],
        out_specs=row(d),
        compiler_params=_params(2),
        name="mixout",
    )(x, mod, o, xc, xc, bg, sga, sgc, conv_w, w_attn, w_conv, w_out)


def kernel(x, c, w_ada, b_ada, norm_ffn1, ffn1_w_gate, ffn1_w_up, ffn1_w_down, norm_mix, w_in, q_norm, k_norm,
           conv_w, w_attn_branch, w_conv_branch, w_out, norm_ffn2, ffn2_w_gate, ffn2_w_up, ffn2_w_down):
    b, s, d = x.shape
    depth = w_ada.shape[0]
    c_pad = jnp.zeros((8, d), F32).at[:b].set(c)
    for l in range(depth):
        mod = _adaln(c_pad, w_ada[l], b_ada[l][None, :])[:b].reshape(b, N_MOD, d)
        x = _ffn(x, mod, norm_ffn1[l][None, :], ffn1_w_gate[l].astype(BF16), ffn1_w_up[l].astype(BF16),
                 ffn1_w_down[l].astype(BF16), mod_row=0)
        res = _proj(x, mod, norm_mix[l][None, :], q_norm[l][None, :], k_norm[l][None, :], w_in[l].astype(BF16))
        o = _attn(res[:3 * N_GROUPS])
        x = _mixout(x, mod, o, *res[3 * N_GROUPS:], conv_w[l], w_attn_branch[l].astype(BF16),
                    w_conv_branch[l].astype(BF16), w_out[l].astype(BF16))
        x = _ffn(x, mod, norm_ffn2[l][None, :], ffn2_w_gate[l].astype(BF16), ffn2_w_up[l].astype(BF16),
                 ffn2_w_down[l].astype(BF16), mod_row=6)
    return x
```

```python
import functools

import jax
import jax.numpy as jnp
from jax import lax
from jax.experimental import pallas as pl
from jax.experimental.pallas import tpu as pltpu

EPS = 1e-6
N_MOD = 9
N_GROUPS = 3
HEADS = 4
HEAD_DIM = 128
DILATIONS = (1, 4, 16)
BAND = 128
CHUNK = BAND * max(DILATIONS)
TILE = 512
CONV_K = 3

V7X_VMEM_BYTES = 64 * 1024 * 1024
VMEM_LIMIT = V7X_VMEM_BYTES - 8 * 1024 * 1024

BF16 = jnp.bfloat16
BF16_SUBLANES = 16
F32 = jnp.float32


def _resident(shape):
    return pl.BlockSpec(shape, lambda *_: (0,) * len(shape), pipeline_mode=pl.Buffered(1))


def _params(n_axes):
    return pltpu.CompilerParams(dimension_semantics=("arbitrary",) * n_axes, vmem_limit_bytes=VMEM_LIMIT)


def _rms_mod(x, gain, shift, scale):
    ms = jnp.mean(x * x, axis=-1, keepdims=True)
    y = x * lax.rsqrt(ms + EPS) * gain
    return y * (1.0 + scale) + shift


def _dot(a, b):
    return jnp.dot(a, b, preferred_element_type=F32)


def _cast_specs(ws, grid):
    n_steps = 1
    for g in grid:
        n_steps *= g
    specs, shapes = [], []
    for w in ws:
        rows, cols = w.shape
        n_bands = n_steps
        while rows % n_bands or (rows // n_bands) % BF16_SUBLANES:
            n_bands //= 2
        per = n_steps // n_bands

        def index(*ids, per=per):
            flat = ids[0]
            for i, g in zip(ids[1:], grid[1:]):
                flat = flat * g + i
            return (flat // per, 0)

        specs.append(pl.BlockSpec((rows // n_bands, cols), index))
        shapes.append(jax.ShapeDtypeStruct(w.shape, BF16))
    return specs, shapes


def _cast_bands(src_refs, dst_refs):
    for src, dst in zip(src_refs, dst_refs):
        dst[...] = src[...].astype(BF16)


def _adaln_kernel(c_ref, w_ref, b_ref, o_ref):
    c = c_ref[...]
    c_act = c * jax.nn.sigmoid(c)
    o_ref[...] = _dot(c_act, w_ref[...]) + b_ref[...]


def _adaln(c_pad, w_ada, b_ada, tn=1152):
    rows, d = c_pad.shape
    n = w_ada.shape[1]
    return pl.pallas_call(
        _adaln_kernel,
        out_shape=jax.ShapeDtypeStruct((rows, n), F32),
        grid=(n // tn,),
        in_specs=[
            pl.BlockSpec((rows, d), lambda j: (0, 0)),
            pl.BlockSpec((d, tn), lambda j: (0, j)),
            pl.BlockSpec((1, tn), lambda j: (0, j)),
        ],
        out_specs=pl.BlockSpec((rows, tn), lambda j: (0, j)),
        compiler_params=_params(1),
        name="adaln",
    )(c_pad, w_ada, b_ada)


def _ffn_kernel(mod_row, n_cast, x_ref, mod_ref, g_ref, wg_ref, wu_ref, wd_ref, *rest):
    o_ref = rest[n_cast]
    _cast_bands(rest[:n_cast], rest[n_cast + 1:])
    x = x_ref[...]
    shift = mod_ref[mod_row:mod_row + 1, :]
    scale = mod_ref[mod_row + 1:mod_row + 2, :]
    gate = mod_ref[mod_row + 2:mod_row + 3, :]
    h = _rms_mod(x, g_ref[...], shift, scale).astype(BF16)
    a = _dot(h, wg_ref[...])
    u = _dot(h, wu_ref[...])
    act = (a * jax.nn.sigmoid(a) * u).astype(BF16)
    y = _dot(act, wd_ref[...])
    o_ref[...] = x + 0.5 * gate * y


def _ffn(x, mod, gain, w_gate, w_up, w_down, mod_row, cast_ws=(), tm=512):
    b, s, d = x.shape
    f = w_gate.shape[1]
    grid = (b, s // tm)
    cast_specs, cast_shapes = _cast_specs(cast_ws, grid)
    row = pl.BlockSpec((None, tm, d), lambda bi, i: (bi, i, 0))
    return pl.pallas_call(
        functools.partial(_ffn_kernel, mod_row, len(cast_ws)),
        out_shape=[jax.ShapeDtypeStruct(x.shape, F32)] + cast_shapes,
        grid=grid,
        in_specs=[
            row,
            pl.BlockSpec((None, N_MOD, d), lambda bi, i: (bi, 0, 0)),
            _resident((1, d)),
            _resident((d, f)),
            _resident((d, f)),
            _resident((f, d)),
        ] + cast_specs,
        out_specs=[row] + cast_specs,
        compiler_params=_params(2),
        name="ffn",
    )(x, mod, gain, w_gate, w_up, w_down, *cast_ws)


def _store_by_residue(slab, levels, out_ref, col, scr_a, scr_b):
    scr_a[...] = slab
    q = TILE // 4
    if levels == 1:
        for r in range(4):
            out_ref[r * q:(r + 1) * q, col:col + HEAD_DIM] = scr_a[pl.ds(r, q, stride=4), :].astype(BF16)
        return
    for r in range(4):
        scr_b[r * q:(r + 1) * q, :] = scr_a[pl.ds(r, q, stride=4), :]
    qq = q // 4
    for r in range(4):
        for r2 in range(4):
            p = r * 4 + r2
            out_ref[p * qq:(p + 1) * qq, col:col + HEAD_DIM] = scr_b[pl.ds(r * q + r2, qq, stride=4), :].astype(BF16)


def _proj_kernel(n_cast, x_ref, mod_ref, g_ref, qn_ref, kn_ref, w_ref, *rest):
    cast_in, rest = rest[:n_cast], rest[n_cast:]
    qkv_refs = rest[:3 * N_GROUPS]
    xc_ref, bg_ref, sga_ref, sgc_ref = rest[3 * N_GROUPS:3 * N_GROUPS + 4]
    cast_out = rest[3 * N_GROUPS + 4:3 * N_GROUPS + 4 + n_cast]
    scr_a, scr_b = rest[3 * N_GROUPS + 4 + n_cast:]
    _cast_bands(cast_in, cast_out)
    d = g_ref.shape[-1]
    aw = HEADS * HEAD_DIM
    h = _rms_mod(x_ref[...], g_ref[...], mod_ref[3:4, :], mod_ref[4:5, :]).astype(BF16)
    gains = (qn_ref[...] * HEAD_DIM ** -0.5, kn_ref[...], None)
    for j in range(3):
        for g in range(N_GROUPS):
            c0 = (j * N_GROUPS + g) * aw
            t = _dot(h, w_ref[:, c0:c0 + aw])
            out_ref = qkv_refs[3 * g + j]
            for hd in range(HEADS):
                col = hd * HEAD_DIM
                slab = t[:, col:col + HEAD_DIM]
                if gains[j] is not None:
                    ms = jnp.mean(slab * slab, axis=-1, keepdims=True)
                    slab = slab * lax.rsqrt(ms + EPS) * gains[j]
                if g == 0:
                    out_ref[:, col:col + HEAD_DIM] = slab.astype(BF16)
                else:
                    _store_by_residue(slab, g, out_ref, col, scr_a.at[hd], scr_b.at[hd])
    o = 3 * N_GROUPS * aw
    u = _dot(h, w_ref[:, o:o + d])
    cg = _dot(h, w_ref[:, o + 2 * d:o + 3 * d])
    xc_ref[...] = (cg * u).astype(BF16)
    bg_ref[...] = _dot(h, w_ref[:, o + d:o + 2 * d]).astype(BF16)
    sga_ref[...] = jax.nn.sigmoid(_dot(h, w_ref[:, o + 3 * d:o + 4 * d])).astype(BF16)
    sgc_ref[...] = jax.nn.sigmoid(_dot(h, w_ref[:, o + 4 * d:o + 5 * d])).astype(BF16)


def _proj(x, mod, gain, q_norm, k_norm, w, cast_ws=()):
    b, s, d = x.shape
    aw = HEADS * HEAD_DIM
    grid = (b, s // TILE)
    cast_specs, cast_shapes = _cast_specs(cast_ws, grid)
    row = lambda width: pl.BlockSpec((None, TILE, width), lambda bi, i: (bi, i, 0))
    return pl.pallas_call(
        functools.partial(_proj_kernel, len(cast_ws)),
        out_shape=([jax.ShapeDtypeStruct((b, s, aw), BF16)] * (3 * N_GROUPS) + [jax.ShapeDtypeStruct((b, s, d), BF16)] * 4
                   + cast_shapes),
        grid=grid,
        in_specs=[
            row(d),
            pl.BlockSpec((None, N_MOD, d), lambda bi, i: (bi, 0, 0)),
            _resident((1, d)),
            _resident((1, HEAD_DIM)),
            _resident((1, HEAD_DIM)),
            _resident(w.shape),
        ] + cast_specs,
        out_specs=[row(aw)] * (3 * N_GROUPS) + [row(d)] * 4 + cast_specs,
        scratch_shapes=[pltpu.VMEM((HEADS, TILE, HEAD_DIM), F32), pltpu.VMEM((HEADS, TILE, HEAD_DIM), F32)],
        compiler_params=_params(2),
        name="proj",
    )(x, mod, gain, q_norm, k_norm, w, *cast_ws)


def _attn_block(q, kcat, vcat, bias):
    s = lax.dot_general(q, kcat, (((1,), (1,)), ((), ())), preferred_element_type=F32) + bias
    m = jnp.max(s, axis=-1, keepdims=True)
    p = jnp.exp(s - m)
    den = jnp.sum(p, axis=-1, keepdims=True)
    o = _dot(p.astype(BF16), vcat) / den
    return o, m + jnp.log(den)


def _band_rows(dil, r, jb):
    per_tile = TILE // dil
    if dil == 1:
        return [(jb * BAND, BAND)]
    pos = r if dil == 4 else (r % 4) * 4 + r // 4
    tiles = BAND // per_tile
    return [((jb * tiles + t) * TILE + pos * per_tile, per_tile) for t in range(tiles)]


def _gather(ref, ranges):
    parts = [ref[a:a + n, :] for a, n in ranges]
    return parts[0] if len(parts) == 1 else jnp.concatenate(parts, axis=0)


def _attn_kernel(*refs):
    ins = refs[:3 * N_GROUPS]
    out_ref = refs[3 * N_GROUPS]
    scr = refs[3 * N_GROUPS + 1:]
    kprev, vprev = scr[0:N_GROUPS], scr[N_GROUPS:2 * N_GROUPS]
    o_nat, l_nat = scr[2 * N_GROUPS:3 * N_GROUPS], scr[3 * N_GROUPS:4 * N_GROUPS]
    c = pl.program_id(2)

    @pl.when(c == 0)
    def _():
        for ref in kprev + vprev:
            ref[...] = jnp.zeros(ref.shape, ref.dtype)

    qi = lax.broadcasted_iota(jnp.int32, (BAND, 2 * BAND), 0)
    kj = lax.broadcasted_iota(jnp.int32, (BAND, 2 * BAND), 1)
    in_prev = (kj < BAND) & (kj >= qi)
    in_cur = (kj >= BAND) & (kj - BAND <= qi)
    neg = jnp.float32(-jnp.inf)
    bias_in = jnp.where(in_prev | in_cur, 0.0, neg)
    bias_first = jnp.where(in_cur, 0.0, neg)
    bias_edge = jnp.where(c > 0, bias_in, bias_first)

    for g, dil in enumerate(DILATIONS):
        q_ref, k_ref, v_ref = ins[3 * g:3 * g + 3]
        blocks = CHUNK // dil // BAND
        carry = [rg for r in range(dil) for rg in _band_rows(dil, r, blocks - 1)]
        for r in range(dil):
            for jb in range(blocks):
                cur = _band_rows(dil, r, jb)
                q = _gather(q_ref, cur)
                if jb == 0:
                    base = r * BAND
                    k_prev = kprev[g][base:base + BAND, :]
                    v_prev = vprev[g][base:base + BAND, :]
                    bias = bias_edge
                else:
                    prev = _band_rows(dil, r, jb - 1)
                    k_prev, v_prev = _gather(k_ref, prev), _gather(v_ref, prev)
                    bias = bias_in
                kcat = jnp.concatenate([k_prev, _gather(k_ref, cur)], axis=0)
                vcat = jnp.concatenate([v_prev, _gather(v_ref, cur)], axis=0)
                o, lse = _attn_block(q, kcat, vcat, bias)
                if dil == 1:
                    rows = pl.ds(jb * BAND, BAND)
                else:
                    rows = pl.ds(jb * BAND * dil + r, BAND, stride=dil)
                o_nat[g][rows, :] = o
                l_nat[g][rows, :] = jnp.broadcast_to(lse, (BAND, HEAD_DIM))
        off = 0
        for a, n in carry:
            kprev[g][off:off + n, :] = k_ref[a:a + n, :]
            vprev[g][off:off + n, :] = v_ref[a:a + n, :]
            off += n

    step = 256
    for a in range(0, CHUNK, step):
        ls = [l_nat[g][a:a + step, :] for g in range(N_GROUPS)]
        m = jnp.maximum(jnp.maximum(ls[0], ls[1]), ls[2])
        es = [jnp.exp(l - m) for l in ls]
        num = es[0] * o_nat[0][a:a + step, :] + es[1] * o_nat[1][a:a + step, :] + es[2] * o_nat[2][a:a + step, :]
        out_ref[a:a + step, :] = (num / (es[0] + es[1] + es[2])).astype(BF16)


def _attn(qkv):
    b, s, aw = qkv[0].shape
    blk = pl.BlockSpec((None, CHUNK, HEAD_DIM), lambda bi, h, c: (bi, c, h))
    carry_rows = [BAND * dil for dil in DILATIONS]
    return pl.pallas_call(
        _attn_kernel,
        out_shape=jax.ShapeDtypeStruct((b, s, aw), BF16),
        grid=(b, HEADS, s // CHUNK),
        in_specs=[blk] * len(qkv),
        out_specs=blk,
        scratch_shapes=(
            [pltpu.VMEM((n, HEAD_DIM), BF16) for n in carry_rows] * 2
            + [pltpu.VMEM((CHUNK, HEAD_DIM), F32)] * (2 * N_GROUPS)),
        compiler_params=_params(3),
        name="attn",
    )(*qkv)


def _mixout_kernel(x_ref, mod_ref, o_ref, xc_ref, halo_ref, bg_ref, sga_ref, sgc_ref,
                   cw_ref, wa_ref, wc_ref, wo_ref, out_ref):
    tm = x_ref.shape[0]
    i = pl.program_id(1)
    y_attn = _dot(o_ref[...], wa_ref[...])
    xc = xc_ref[...].astype(F32)
    halo = jnp.where(i > 0, halo_ref[...].astype(F32), 0.0)
    ext = jnp.concatenate([halo, xc], axis=0)
    pad = halo.shape[0]
    conv = xc * cw_ref[CONV_K - 1:CONV_K, :]
    for j in range(CONV_K - 1):
        back = CONV_K - 1 - j
        conv = conv + ext[pad - back:pad - back + tm, :] * cw_ref[j:j + 1, :]
    y_conv = _dot((bg_ref[...].astype(F32) * conv).astype(BF16), wc_ref[...])
    merged = (sga_ref[...].astype(F32) * y_attn + sgc_ref[...].astype(F32) * y_conv).astype(BF16)
    out_ref[...] = x_ref[...] + mod_ref[5:6, :] * _dot(merged, wo_ref[...])


def _mixout(x, mod, o, xc, bg, sga, sgc, conv_w, w_attn, w_conv, w_out, tm=512):
    b, s, d = x.shape
    halo_rows = 16
    row = lambda width: pl.BlockSpec((None, tm, width), lambda bi, i: (bi, i, 0))
    halo = pl.BlockSpec((None, halo_rows, d),
                        lambda bi, i: (bi, jnp.maximum(i * (tm // halo_rows) - 1, 0), 0))
    return pl.pallas_call(
        _mixout_kernel,
        out_shape=jax.ShapeDtypeStruct(x.shape, F32),
        grid=(b, s // tm),
        in_specs=[row(d), pl.BlockSpec((None, N_MOD, d), lambda bi, i: (bi, 0, 0)), row(o.shape[-1]),
                  row(d), halo, row(d), row(d), row(d),
                  _resident(conv_w.shape), _resident(w_attn.shape), _resident(w_conv.shape), _resident(w_out.shape)],
        out_specs=row(d),
        compiler_params=_params(2),
        name="mixout",
    )(x, mod, o, xc, xc, bg, sga, sgc, conv_w, w_attn, w_conv, w_out)


def kernel(x, c, w_ada, b_ada, norm_ffn1, ffn1_w_gate, ffn1_w_up, ffn1_w_down, norm_mix, w_in, q_norm, k_norm,
           conv_w, w_attn_branch, w_conv_branch, w_out, norm_ffn2, ffn2_w_gate, ffn2_w_up, ffn2_w_down):
    b, s, d = x.shape
    depth = w_ada.shape[0]
    c_pad = jnp.zeros((8, d), F32).at[:b].set(c)
    for l in range(depth):
        mod = _adaln(c_pad, w_ada[l], b_ada[l][None, :])[:b].reshape(b, N_MOD, d)
        x, w_in_b = _ffn(x, mod, norm_ffn1[l][None, :], ffn1_w_gate[l].astype(BF16), ffn1_w_up[l].astype(BF16),
                         ffn1_w_down[l].astype(BF16), mod_row=0, cast_ws=(w_in[l],))
        res = _proj(x, mod, norm_mix[l][None, :], q_norm[l][None, :], k_norm[l][None, :], w_in_b,
                    cast_ws=(w_attn_branch[l], w_conv_branch[l], w_out[l], ffn2_w_gate[l], ffn2_w_up[l], ffn2_w_down[l]))
        n_qkv = 3 * N_GROUPS
        o = _attn(res[:n_qkv])
        w_attn_b, w_conv_b, w_out_b, w_gate_b, w_up_b, w_down_b = res[n_qkv + 4:]
        x = _mixout(x, mod, o, *res[n_qkv:n_qkv + 4], conv_w[l], w_attn_b, w_conv_b, w_out_b)
        x, = _ffn(x, mod, norm_ffn2[l][None, :], w_gate_b, w_up_b, w_down_b, mod_row=6)
    return x
```

```python
import functools

import jax
import jax.numpy as jnp
from jax import lax
from jax.experimental import pallas as pl
from jax.experimental.pallas import tpu as pltpu

EPS = 1e-6
N_MOD = 9
N_GROUPS = 3
HEADS = 4
HEAD_DIM = 128
DILATIONS = (1, 4, 16)
BAND = 128
CHUNK = BAND * max(DILATIONS)
TILE = 512
CONV_K = 3
LOG2_E = 1.4426950408889634

V7X_VMEM_BYTES = 64 * 1024 * 1024
VMEM_LIMIT = V7X_VMEM_BYTES - 8 * 1024 * 1024

BF16 = jnp.bfloat16
BF16_SUBLANES = 16
F32 = jnp.float32


def _resident(shape):
    return pl.BlockSpec(shape, lambda *_: (0,) * len(shape), pipeline_mode=pl.Buffered(1))


def _params(n_axes):
    return pltpu.CompilerParams(dimension_semantics=("arbitrary",) * n_axes, vmem_limit_bytes=VMEM_LIMIT)


def _rms_mod(x, gain, shift, scale):
    ms = jnp.mean(x * x, axis=-1, keepdims=True)
    y = x * lax.rsqrt(ms + EPS) * gain
    return y * (1.0 + scale) + shift


def _dot(a, b):
    return jnp.dot(a, b, preferred_element_type=F32)


def _cast_specs(ws, grid):
    n_steps = 1
    for g in grid:
        n_steps *= g
    specs, shapes = [], []
    for w in ws:
        rows, cols = w.shape
        n_bands = n_steps
        while rows % n_bands or (rows // n_bands) % BF16_SUBLANES:
            n_bands //= 2
        per = n_steps // n_bands

        def index(*ids, per=per):
            flat = ids[0]
            for i, g in zip(ids[1:], grid[1:]):
                flat = flat * g + i
            return (flat // per, 0)

        specs.append(pl.BlockSpec((rows // n_bands, cols), index))
        shapes.append(jax.ShapeDtypeStruct(w.shape, BF16))
    return specs, shapes


def _cast_bands(src_refs, dst_refs):
    for src, dst in zip(src_refs, dst_refs):
        dst[...] = src[...].astype(BF16)


def _adaln_kernel(c_ref, w_ref, b_ref, o_ref):
    c = c_ref[...]
    c_act = c * jax.nn.sigmoid(c)
    o_ref[...] = _dot(c_act, w_ref[...]) + b_ref[...]


def _adaln(c_pad, w_ada, b_ada, tn=1152):
    rows, d = c_pad.shape
    n = w_ada.shape[1]
    return pl.pallas_call(
        _adaln_kernel,
        out_shape=jax.ShapeDtypeStruct((rows, n), F32),
        grid=(n // tn,),
        in_specs=[
            pl.BlockSpec((rows, d), lambda j: (0, 0)),
            pl.BlockSpec((d, tn), lambda j: (0, j)),
            pl.BlockSpec((1, tn), lambda j: (0, j)),
        ],
        out_specs=pl.BlockSpec((rows, tn), lambda j: (0, j)),
        compiler_params=_params(1),
        name="adaln",
    )(c_pad, w_ada, b_ada)


def _ffn_kernel(mod_row, n_cast, x_ref, mod_ref, g_ref, wg_ref, wu_ref, wd_ref, *rest):
    o_ref = rest[n_cast]
    _cast_bands(rest[:n_cast], rest[n_cast + 1:])
    shift = mod_ref[mod_row:mod_row + 1, :]
    scale = mod_ref[mod_row + 1:mod_row + 2, :]
    gate = mod_ref[mod_row + 2:mod_row + 3, :]
    half = x_ref.shape[0] // 2
    for r0 in (0, half):
        x = x_ref[r0:r0 + half, :]
        h = _rms_mod(x, g_ref[...], shift, scale).astype(BF16)
        a = _dot(h, wg_ref[...])
        u = _dot(h, wu_ref[...])
        act = (a * jax.nn.sigmoid(a) * u).astype(BF16)
        y = _dot(act, wd_ref[...])
        o_ref[r0:r0 + half, :] = x + 0.5 * gate * y


def _ffn(x, mod, gain, w_gate, w_up, w_down, mod_row, cast_ws=(), tm=512):
    b, s, d = x.shape
    f = w_gate.shape[1]
    grid = (b, s // tm)
    cast_specs, cast_shapes = _cast_specs(cast_ws, grid)
    row = pl.BlockSpec((None, tm, d), lambda bi, i: (bi, i, 0))
    return pl.pallas_call(
        functools.partial(_ffn_kernel, mod_row, len(cast_ws)),
        out_shape=[jax.ShapeDtypeStruct(x.shape, F32)] + cast_shapes,
        grid=grid,
        in_specs=[
            row,
            pl.BlockSpec((None, N_MOD, d), lambda bi, i: (bi, 0, 0)),
            _resident((1, d)),
            _resident((d, f)),
            _resident((d, f)),
            _resident((f, d)),
        ] + cast_specs,
        out_specs=[row] + cast_specs,
        compiler_params=_params(2),
        name="ffn",
    )(x, mod, gain, w_gate, w_up, w_down, *cast_ws)


def _store_by_residue(slab, levels, out_ref, col, scr_a, scr_b):
    scr_a[...] = slab
    q = TILE // 4
    if levels == 1:
        for r in range(4):
            out_ref[r * q:(r + 1) * q, col:col + HEAD_DIM] = scr_a[pl.ds(r, q, stride=4), :].astype(BF16)
        return
    for r in range(4):
        scr_b[r * q:(r + 1) * q, :] = scr_a[pl.ds(r, q, stride=4), :]
    qq = q // 4
    for r in range(4):
        for r2 in range(4):
            p = r * 4 + r2
            out_ref[p * qq:(p + 1) * qq, col:col + HEAD_DIM] = scr_b[pl.ds(r * q + r2, qq, stride=4), :].astype(BF16)


def _proj_kernel(n_cast, x_ref, mod_ref, g_ref, qn_ref, kn_ref, w_ref, *rest):
    cast_in, rest = rest[:n_cast], rest[n_cast:]
    qkv_refs = rest[:3 * N_GROUPS]
    xc_ref, bg_ref, sga_ref, sgc_ref = rest[3 * N_GROUPS:3 * N_GROUPS + 4]
    cast_out = rest[3 * N_GROUPS + 4:3 * N_GROUPS + 4 + n_cast]
    scr_a, scr_b = rest[3 * N_GROUPS + 4 + n_cast:]
    _cast_bands(cast_in, cast_out)
    d = g_ref.shape[-1]
    aw = HEADS * HEAD_DIM
    h = _rms_mod(x_ref[...], g_ref[...], mod_ref[3:4, :], mod_ref[4:5, :]).astype(BF16)
    gains = (qn_ref[...] * (HEAD_DIM ** -0.5 * LOG2_E), kn_ref[...], None)
    for j in range(3):
        for g in range(N_GROUPS):
            c0 = (j * N_GROUPS + g) * aw
            t = _dot(h, w_ref[:, c0:c0 + aw])
            out_ref = qkv_refs[3 * g + j]
            for hd in range(HEADS):
                col = hd * HEAD_DIM
                slab = t[:, col:col + HEAD_DIM]
                if gains[j] is not None:
                    ms = jnp.mean(slab * slab, axis=-1, keepdims=True)
                    slab = slab * lax.rsqrt(ms + EPS) * gains[j]
                if g == 0:
                    out_ref[:, col:col + HEAD_DIM] = slab.astype(BF16)
                else:
                    _store_by_residue(slab, g, out_ref, col, scr_a.at[hd], scr_b.at[hd])
    o = 3 * N_GROUPS * aw
    u = _dot(h, w_ref[:, o:o + d])
    cg = _dot(h, w_ref[:, o + 2 * d:o + 3 * d])
    xc_ref[...] = (cg * u).astype(BF16)
    bg_ref[...] = _dot(h, w_ref[:, o + d:o + 2 * d]).astype(BF16)
    sga_ref[...] = jax.nn.sigmoid(_dot(h, w_ref[:, o + 3 * d:o + 4 * d])).astype(BF16)
    sgc_ref[...] = jax.nn.sigmoid(_dot(h, w_ref[:, o + 4 * d:o + 5 * d])).astype(BF16)


def _proj(x, mod, gain, q_norm, k_norm, w, cast_ws=()):
    b, s, d = x.shape
    aw = HEADS * HEAD_DIM
    grid = (b, s // TILE)
    cast_specs, cast_shapes = _cast_specs(cast_ws, grid)
    row = lambda width: pl.BlockSpec((None, TILE, width), lambda bi, i: (bi, i, 0))
    return pl.pallas_call(
        functools.partial(_proj_kernel, len(cast_ws)),
        out_shape=([jax.ShapeDtypeStruct((b, s, aw), BF16)] * (3 * N_GROUPS) + [jax.ShapeDtypeStruct((b, s, d), BF16)] * 4
                   + cast_shapes),
        grid=grid,
        in_specs=[
            row(d),
            pl.BlockSpec((None, N_MOD, d), lambda bi, i: (bi, 0, 0)),
            _resident((1, d)),
            _resident((1, HEAD_DIM)),
            _resident((1, HEAD_DIM)),
            _resident(w.shape),
        ] + cast_specs,
        out_specs=[row(aw)] * (3 * N_GROUPS) + [row(d)] * 4 + cast_specs,
        scratch_shapes=[pltpu.VMEM((HEADS, TILE, HEAD_DIM), F32), pltpu.VMEM((HEADS, TILE, HEAD_DIM), F32)],
        compiler_params=_params(2),
        name="proj",
    )(x, mod, gain, q_norm, k_norm, w, *cast_ws)


def _attn_block(q, kcat, vcat, bias):
    s = lax.dot_general(q, kcat, (((1,), (1,)), ((), ())), preferred_element_type=F32) + bias
    m = jnp.max(s, axis=-1, keepdims=True)
    p = jnp.exp2(s - m)
    od = _dot(p.astype(BF16), jnp.concatenate([vcat, jnp.ones_like(vcat)], axis=1))
    den = od[:, HEAD_DIM:]
    return od[:, :HEAD_DIM] / den, m + jnp.log2(den)


def _band_rows(dil, r, jb):
    per_tile = TILE // dil
    if dil == 1:
        return [(jb * BAND, BAND)]
    pos = r if dil == 4 else (r % 4) * 4 + r // 4
    tiles = BAND // per_tile
    return [((jb * tiles + t) * TILE + pos * per_tile, per_tile) for t in range(tiles)]


def _gather(ref, ranges):
    parts = [ref[a:a + n, :] for a, n in ranges]
    return parts[0] if len(parts) == 1 else jnp.concatenate(parts, axis=0)


def _attn_kernel(*refs):
    ins = refs[:3 * N_GROUPS]
    out_ref = refs[3 * N_GROUPS]
    scr = refs[3 * N_GROUPS + 1:]
    kprev, vprev = scr[0:N_GROUPS], scr[N_GROUPS:2 * N_GROUPS]
    o_nat, l_nat = scr[2 * N_GROUPS:3 * N_GROUPS], scr[3 * N_GROUPS:4 * N_GROUPS]
    c = pl.program_id(2)

    @pl.when(c == 0)
    def _():
        for ref in kprev + vprev:
            ref[...] = jnp.zeros(ref.shape, ref.dtype)

    qi = lax.broadcasted_iota(jnp.int32, (BAND, 2 * BAND), 0)
    kj = lax.broadcasted_iota(jnp.int32, (BAND, 2 * BAND), 1)
    in_prev = (kj < BAND) & (kj >= qi)
    in_cur = (kj >= BAND) & (kj - BAND <= qi)
    neg = jnp.float32(-jnp.inf)
    bias_in = jnp.where(in_prev | in_cur, 0.0, neg)
    bias_first = jnp.where(in_cur, 0.0, neg)
    bias_edge = jnp.where(c > 0, bias_in, bias_first)

    for g, dil in enumerate(DILATIONS):
        q_ref, k_ref, v_ref = ins[3 * g:3 * g + 3]
        blocks = CHUNK // dil // BAND
        carry = [rg for r in range(dil) for rg in _band_rows(dil, r, blocks - 1)]
        for r in range(dil):
            for jb in range(blocks):
                cur = _band_rows(dil, r, jb)
                q = _gather(q_ref, cur)
                if jb == 0:
                    base = r * BAND
                    k_prev = kprev[g][base:base + BAND, :]
                    v_prev = vprev[g][base:base + BAND, :]
                    bias = bias_edge
                else:
                    prev = _band_rows(dil, r, jb - 1)
                    k_prev, v_prev = _gather(k_ref, prev), _gather(v_ref, prev)
                    bias = bias_in
                kcat = jnp.concatenate([k_prev, _gather(k_ref, cur)], axis=0)
                vcat = jnp.concatenate([v_prev, _gather(v_ref, cur)], axis=0)
                o, lse = _attn_block(q, kcat, vcat, bias)
                if dil == 1:
                    rows = pl.ds(jb * BAND, BAND)
                else:
                    rows = pl.ds(jb * BAND * dil + r, BAND, stride=dil)
                o_nat[g][rows, :] = o
                l_nat[g][rows, :] = lse
        off = 0
        for a, n in carry:
            kprev[g][off:off + n, :] = k_ref[a:a + n, :]
            vprev[g][off:off + n, :] = v_ref[a:a + n, :]
            off += n

    step = 256
    for a in range(0, CHUNK, step):
        ls = [l_nat[g][a:a + step, :] for g in range(N_GROUPS)]
        m = jnp.maximum(jnp.maximum(ls[0], ls[1]), ls[2])
        es = [jnp.exp2(l - m) for l in ls]
        num = es[0] * o_nat[0][a:a + step, :] + es[1] * o_nat[1][a:a + step, :] + es[2] * o_nat[2][a:a + step, :]
        out_ref[a:a + step, :] = (num / (es[0] + es[1] + es[2])).astype(BF16)


def _attn(qkv):
    b, s, aw = qkv[0].shape
    blk = pl.BlockSpec((None, CHUNK, HEAD_DIM), lambda bi, h, c: (bi, c, h))
    carry_rows = [BAND * dil for dil in DILATIONS]
    return pl.pallas_call(
        _attn_kernel,
        out_shape=jax.ShapeDtypeStruct((b, s, aw), BF16),
        grid=(b, HEADS, s // CHUNK),
        in_specs=[blk] * len(qkv),
        out_specs=blk,
        scratch_shapes=(
            [pltpu.VMEM((n, HEAD_DIM), BF16) for n in carry_rows] * 2
            + [pltpu.VMEM((CHUNK, HEAD_DIM), F32)] * (2 * N_GROUPS)),
        compiler_params=_params(3),
        name="attn",
    )(*qkv)


def _mixout_kernel(x_ref, mod_ref, o_ref, xc_ref, halo_ref, bg_ref, sga_ref, sgc_ref,
                   cw_ref, wa_ref, wc_ref, wo_ref, out_ref):
    tm = x_ref.shape[0]
    i = pl.program_id(1)
    y_attn = _dot(o_ref[...], wa_ref[...])
    xc = xc_ref[...].astype(F32)
    halo = jnp.where(i > 0, halo_ref[...].astype(F32), 0.0)
    ext = jnp.concatenate([halo, xc], axis=0)
    pad = halo.shape[0]
    conv = xc * cw_ref[CONV_K - 1:CONV_K, :]
    for j in range(CONV_K - 1):
        back = CONV_K - 1 - j
        conv = conv + ext[pad - back:pad - back + tm, :] * cw_ref[j:j + 1, :]
    y_conv = _dot((bg_ref[...].astype(F32) * conv).astype(BF16), wc_ref[...])
    merged = (sga_ref[...].astype(F32) * y_attn + sgc_ref[...].astype(F32) * y_conv).astype(BF16)
    out_ref[...] = x_ref[...] + mod_ref[5:6, :] * _dot(merged, wo_ref[...])


def _mixout(x, mod, o, xc, bg, sga, sgc, conv_w, w_attn, w_conv, w_out, tm=512):
    b, s, d = x.shape
    halo_rows = 16
    row = lambda width: pl.BlockSpec((None, tm, width), lambda bi, i: (bi, i, 0))
    halo = pl.BlockSpec((None, halo_rows, d),
                        lambda bi, i: (bi, jnp.maximum(i * (tm // halo_rows) - 1, 0), 0))
    return pl.pallas_call(
        _mixout_kernel,
        out_shape=jax.ShapeDtypeStruct(x.shape, F32),
        grid=(b, s // tm),
        in_specs=[row(d), pl.BlockSpec((None, N_MOD, d), lambda bi, i: (bi, 0, 0)), row(o.shape[-1]),
                  row(d), halo, row(d), row(d), row(d),
                  _resident(conv_w.shape), _resident(w_attn.shape), _resident(w_conv.shape), _resident(w_out.shape)],
        out_specs=row(d),
        compiler_params=_params(2),
        name="mixout",
    )(x, mod, o, xc, xc, bg, sga, sgc, conv_w, w_attn, w_conv, w_out)


def kernel(x, c, w_ada, b_ada, norm_ffn1, ffn1_w_gate, ffn1_w_up, ffn1_w_down, norm_mix, w_in, q_norm, k_norm,
           conv_w, w_attn_branch, w_conv_branch, w_out, norm_ffn2, ffn2_w_gate, ffn2_w_up, ffn2_w_down):
    b, s, d = x.shape
    depth = w_ada.shape[0]
    c_pad = jnp.zeros((8, d), F32).at[:b].set(c)
    for l in range(depth):
        mod = _adaln(c_pad, w_ada[l], b_ada[l][None, :])[:b].reshape(b, N_MOD, d)
        x, w_in_b = _ffn(x, mod, norm_ffn1[l][None, :], ffn1_w_gate[l].astype(BF16), ffn1_w_up[l].astype(BF16),
                         ffn1_w_down[l].astype(BF16), mod_row=0, cast_ws=(w_in[l],))
        res = _proj(x, mod, norm_mix[l][None, :], q_norm[l][None, :], k_norm[l][None, :], w_in_b,
                    cast_ws=(w_attn_branch[l], w_conv_branch[l], w_out[l], ffn2_w_gate[l], ffn2_w_up[l], ffn2_w_down[l]))
        n_qkv = 3 * N_GROUPS
        o = _attn(res[:n_qkv])
        w_attn_b, w_conv_b, w_out_b, w_gate_b, w_up_b, w_down_b = res[n_qkv + 4:]
        x = _mixout(x, mod, o, *res[n_qkv:n_qkv + 4], conv_w[l], w_attn_b, w_conv_b, w_out_b)
        x, = _ffn(x, mod, norm_ffn2[l][None, :], w_gate_b, w_up_b, w_down_b, mod_row=6)
    return x
```

```python
import functools

import jax
import jax.numpy as jnp
from jax import lax
from jax.experimental import pallas as pl
from jax.experimental.pallas import tpu as pltpu

EPS = 1e-6
N_MOD = 9
N_GROUPS = 3
HEADS = 4
HEAD_DIM = 128
DILATIONS = (1, 4, 16)
BAND = 128
CHUNK = BAND * max(DILATIONS)
TILE = 512
CONV_K = 3
LOG2_E = 1.4426950408889634

V7X_VMEM_BYTES = 64 * 1024 * 1024
VMEM_LIMIT = V7X_VMEM_BYTES - 8 * 1024 * 1024

BF16 = jnp.bfloat16
BF16_SUBLANES = 16
F32 = jnp.float32


def _resident(shape):
    return pl.BlockSpec(shape, lambda *_: (0,) * len(shape), pipeline_mode=pl.Buffered(1))


def _params(n_axes):
    return pltpu.CompilerParams(dimension_semantics=("arbitrary",) * n_axes, vmem_limit_bytes=VMEM_LIMIT)


def _rms_mod(x, gain, shift, scale):
    ms = jnp.mean(x * x, axis=-1, keepdims=True)
    y = x * lax.rsqrt(ms + EPS) * gain
    return y * (1.0 + scale) + shift


def _dot(a, b):
    return jnp.dot(a, b, preferred_element_type=F32)


def _cast_specs(ws, grid):
    n_steps = 1
    for g in grid:
        n_steps *= g
    specs, shapes = [], []
    for w in ws:
        rows, cols = w.shape
        n_bands = n_steps
        while rows % n_bands or (rows // n_bands) % BF16_SUBLANES:
            n_bands //= 2
        per = n_steps // n_bands

        def index(*ids, per=per):
            flat = ids[0]
            for i, g in zip(ids[1:], grid[1:]):
                flat = flat * g + i
            return (flat // per, 0)

        specs.append(pl.BlockSpec((rows // n_bands, cols), index))
        shapes.append(jax.ShapeDtypeStruct(w.shape, BF16))
    return specs, shapes


def _cast_bands(src_refs, dst_refs):
    for src, dst in zip(src_refs, dst_refs):
        dst[...] = src[...].astype(BF16)


def _adaln_kernel(c_ref, w_ref, b_ref, o_ref):
    c = c_ref[...]
    c_act = c * jax.nn.sigmoid(c)
    o_ref[...] = _dot(c_act, w_ref[...]) + b_ref[...]


def _adaln(c_pad, w_ada, b_ada, tn=1152):
    rows, d = c_pad.shape
    n = w_ada.shape[1]
    return pl.pallas_call(
        _adaln_kernel,
        out_shape=jax.ShapeDtypeStruct((rows, n), F32),
        grid=(n // tn,),
        in_specs=[
            pl.BlockSpec((rows, d), lambda j: (0, 0)),
            pl.BlockSpec((d, tn), lambda j: (0, j)),
            pl.BlockSpec((1, tn), lambda j: (0, j)),
        ],
        out_specs=pl.BlockSpec((rows, tn), lambda j: (0, j)),
        compiler_params=_params(1),
        name="adaln",
    )(c_pad, w_ada, b_ada)


def _ffn_kernel(mod_row, n_cast, x_ref, mod_ref, g_ref, wg_ref, wu_ref, wd_ref, *rest):
    o_ref = rest[n_cast]
    _cast_bands(rest[:n_cast], rest[n_cast + 1:])
    shift = mod_ref[mod_row:mod_row + 1, :]
    scale = mod_ref[mod_row + 1:mod_row + 2, :]
    gate = mod_ref[mod_row + 2:mod_row + 3, :]
    half = x_ref.shape[0] // 2
    for r0 in (0, half):
        x = x_ref[r0:r0 + half, :]
        h = _rms_mod(x, g_ref[...], shift, scale).astype(BF16)
        a = _dot(h, wg_ref[...])
        u = _dot(h, wu_ref[...])
        act = (a * jax.nn.sigmoid(a) * u).astype(BF16)
        y = _dot(act, wd_ref[...])
        o_ref[r0:r0 + half, :] = x + 0.5 * gate * y


def _ffn(x, mod, gain, w_gate, w_up, w_down, mod_row, cast_ws=(), tm=512):
    b, s, d = x.shape
    f = w_gate.shape[1]
    grid = (b, s // tm)
    cast_specs, cast_shapes = _cast_specs(cast_ws, grid)
    row = pl.BlockSpec((None, tm, d), lambda bi, i: (bi, i, 0))
    return pl.pallas_call(
        functools.partial(_ffn_kernel, mod_row, len(cast_ws)),
        out_shape=[jax.ShapeDtypeStruct(x.shape, F32)] + cast_shapes,
        grid=grid,
        in_specs=[
            row,
            pl.BlockSpec((None, N_MOD, d), lambda bi, i: (bi, 0, 0)),
            _resident((1, d)),
            _resident((d, f)),
            _resident((d, f)),
            _resident((f, d)),
        ] + cast_specs,
        out_specs=[row] + cast_specs,
        compiler_params=_params(2),
        name="ffn",
    )(x, mod, gain, w_gate, w_up, w_down, *cast_ws)


def _regroup_rows(src, dst_f32, dst_bf16, seg):
    q = seg // 4
    for base in range(0, TILE, seg):
        for r in range(4):
            lo = base + r * q
            for j in range(src.shape[0]):
                piece = src[j, pl.ds(base + r, q, stride=4), :]
                if dst_f32 is not None:
                    dst_f32[j, lo:lo + q, :] = piece
                dst_bf16[lo:lo + q, j * HEAD_DIM:(j + 1) * HEAD_DIM] = piece.astype(BF16)


def _proj_kernel(n_cast, x_ref, mod_ref, g_ref, qn_ref, kn_ref, w_ref, *rest):
    cast_in, rest = rest[:n_cast], rest[n_cast:]
    qkv_refs = rest[:3 * N_GROUPS]
    xc_ref, bg_ref, sga_ref, sgc_ref = rest[3 * N_GROUPS:3 * N_GROUPS + 4]
    cast_out = rest[3 * N_GROUPS + 4:3 * N_GROUPS + 4 + n_cast]
    scr_a, scr_b, h_scr = rest[3 * N_GROUPS + 4 + n_cast:]
    _cast_bands(cast_in, cast_out)
    d = g_ref.shape[-1]
    aw = HEADS * HEAD_DIM
    hf = _rms_mod(x_ref[...], g_ref[...], mod_ref[3:4, :], mod_ref[4:5, :])
    h = hf.astype(BF16)
    for j in range(d // HEAD_DIM):
        scr_a[j] = hf[:, j * HEAD_DIM:(j + 1) * HEAD_DIM]
    _regroup_rows(scr_a, scr_b, h_scr.at[0], TILE)
    _regroup_rows(scr_b, None, h_scr.at[1], TILE // 4)
    lhs = (h, h_scr[0], h_scr[1])
    o = 3 * N_GROUPS * aw
    sga_ref[...] = jax.nn.sigmoid(_dot(h, w_ref[:, o + 3 * d:o + 4 * d])).astype(BF16)
    sgc_ref[...] = jax.nn.sigmoid(_dot(h, w_ref[:, o + 4 * d:o + 5 * d])).astype(BF16)
    u = _dot(h, w_ref[:, o:o + d])
    cg = _dot(h, w_ref[:, o + 2 * d:o + 3 * d])
    xc_ref[...] = (cg * u).astype(BF16)
    bg_ref[...] = _dot(h, w_ref[:, o + d:o + 2 * d]).astype(BF16)
    gains = (qn_ref[...] * (HEAD_DIM ** -0.5 * LOG2_E), kn_ref[...], None)
    for j in range(3):
        for g in range(N_GROUPS):
            c0 = (j * N_GROUPS + g) * aw
            t = _dot(lhs[g], w_ref[:, c0:c0 + aw])
            out_ref = qkv_refs[3 * g + j]
            for hd in range(HEADS):
                col = hd * HEAD_DIM
                slab = t[:, col:col + HEAD_DIM]
                if gains[j] is not None:
                    ms = jnp.mean(slab * slab, axis=-1, keepdims=True)
                    slab = slab * lax.rsqrt(ms + EPS) * gains[j]
                out_ref[:, col:col + HEAD_DIM] = slab.astype(BF16)


def _proj(x, mod, gain, q_norm, k_norm, w, cast_ws=()):
    b, s, d = x.shape
    aw = HEADS * HEAD_DIM
    grid = (b, s // TILE)
    cast_specs, cast_shapes = _cast_specs(cast_ws, grid)
    row = lambda width: pl.BlockSpec((None, TILE, width), lambda bi, i: (bi, i, 0))
    return pl.pallas_call(
        functools.partial(_proj_kernel, len(cast_ws)),
        out_shape=([jax.ShapeDtypeStruct((b, s, aw), BF16)] * (3 * N_GROUPS) + [jax.ShapeDtypeStruct((b, s, d), BF16)] * 4
                   + cast_shapes),
        grid=grid,
        in_specs=[
            row(d),
            pl.BlockSpec((None, N_MOD, d), lambda bi, i: (bi, 0, 0)),
            _resident((1, d)),
            _resident((1, HEAD_DIM)),
            _resident((1, HEAD_DIM)),
            _resident(w.shape),
        ] + cast_specs,
        out_specs=[row(aw)] * (3 * N_GROUPS) + [row(d)] * 4 + cast_specs,
        scratch_shapes=[pltpu.VMEM((d // HEAD_DIM, TILE, HEAD_DIM), F32), pltpu.VMEM((d // HEAD_DIM, TILE, HEAD_DIM), F32),
                        pltpu.VMEM((2, TILE, d), BF16)],
        compiler_params=_params(2),
        name="proj",
    )(x, mod, gain, q_norm, k_norm, w, *cast_ws)


def _attn_block(q, kcat, vcat, bias):
    s = lax.dot_general(q, kcat, (((1,), (1,)), ((), ())), preferred_element_type=F32) + bias
    m = jnp.max(s, axis=-1, keepdims=True)
    p = jnp.exp2(s - m)
    od = _dot(p.astype(BF16), jnp.concatenate([vcat, jnp.ones_like(vcat)], axis=1))
    den = od[:, HEAD_DIM:]
    return od[:, :HEAD_DIM] / den, m + jnp.log2(den)


def _band_rows(dil, r, jb):
    per_tile = TILE // dil
    if dil == 1:
        return [(jb * BAND, BAND)]
    pos = r if dil == 4 else (r % 4) * 4 + r // 4
    tiles = BAND // per_tile
    return [((jb * tiles + t) * TILE + pos * per_tile, per_tile) for t in range(tiles)]


def _gather(ref, ranges):
    parts = [ref[a:a + n, :] for a, n in ranges]
    return parts[0] if len(parts) == 1 else jnp.concatenate(parts, axis=0)


def _attn_kernel(*refs):
    ins = refs[:3 * N_GROUPS]
    out_ref = refs[3 * N_GROUPS]
    scr = refs[3 * N_GROUPS + 1:]
    kprev, vprev = scr[0:N_GROUPS], scr[N_GROUPS:2 * N_GROUPS]
    o_nat, l_nat = scr[2 * N_GROUPS:3 * N_GROUPS], scr[3 * N_GROUPS:4 * N_GROUPS]
    c = pl.program_id(2)

    @pl.when(c == 0)
    def _():
        for ref in kprev + vprev:
            ref[...] = jnp.zeros(ref.shape, ref.dtype)

    qi = lax.broadcasted_iota(jnp.int32, (BAND, 2 * BAND), 0)
    kj = lax.broadcasted_iota(jnp.int32, (BAND, 2 * BAND), 1)
    in_prev = (kj < BAND) & (kj >= qi)
    in_cur = (kj >= BAND) & (kj - BAND <= qi)
    neg = jnp.float32(-jnp.inf)
    bias_in = jnp.where(in_prev | in_cur, 0.0, neg)
    bias_first = jnp.where(in_cur, 0.0, neg)
    bias_edge = jnp.where(c > 0, bias_in, bias_first)

    for g, dil in enumerate(DILATIONS):
        q_ref, k_ref, v_ref = ins[3 * g:3 * g + 3]
        blocks = CHUNK // dil // BAND
        carry = [rg for r in range(dil) for rg in _band_rows(dil, r, blocks - 1)]
        for r in range(dil):
            for jb in range(blocks):
                cur = _band_rows(dil, r, jb)
                q = _gather(q_ref, cur)
                if jb == 0:
                    base = r * BAND
                    k_prev = kprev[g][base:base + BAND, :]
                    v_prev = vprev[g][base:base + BAND, :]
                    bias = bias_edge
                else:
                    prev = _band_rows(dil, r, jb - 1)
                    k_prev, v_prev = _gather(k_ref, prev), _gather(v_ref, prev)
                    bias = bias_in
                kcat = jnp.concatenate([k_prev, _gather(k_ref, cur)], axis=0)
                vcat = jnp.concatenate([v_prev, _gather(v_ref, cur)], axis=0)
                o, lse = _attn_block(q, kcat, vcat, bias)
                if dil == 1:
                    rows = pl.ds(jb * BAND, BAND)
                else:
                    rows = pl.ds(jb * BAND * dil + r, BAND, stride=dil)
                o_nat[g][rows, :] = o
                l_nat[g][rows, :] = lse
        off = 0
        for a, n in carry:
            kprev[g][off:off + n, :] = k_ref[a:a + n, :]
            vprev[g][off:off + n, :] = v_ref[a:a + n, :]
            off += n

    step = 256
    for a in range(0, CHUNK, step):
        ls = [l_nat[g][a:a + step, :] for g in range(N_GROUPS)]
        m = jnp.maximum(jnp.maximum(ls[0], ls[1]), ls[2])
        es = [jnp.exp2(l - m) for l in ls]
        num = es[0] * o_nat[0][a:a + step, :] + es[1] * o_nat[1][a:a + step, :] + es[2] * o_nat[2][a:a + step, :]
        out_ref[a:a + step, :] = (num / (es[0] + es[1] + es[2])).astype(BF16)


def _attn(qkv):
    b, s, aw = qkv[0].shape
    blk = pl.BlockSpec((None, CHUNK, HEAD_DIM), lambda bi, h, c: (bi, c, h))
    carry_rows = [BAND * dil for dil in DILATIONS]
    return pl.pallas_call(
        _attn_kernel,
        out_shape=jax.ShapeDtypeStruct((b, s, aw), BF16),
        grid=(b, HEADS, s // CHUNK),
        in_specs=[blk] * len(qkv),
        out_specs=blk,
        scratch_shapes=(
            [pltpu.VMEM((n, HEAD_DIM), BF16) for n in carry_rows] * 2
            + [pltpu.VMEM((CHUNK, HEAD_DIM), F32)] * (2 * N_GROUPS)),
        compiler_params=_params(3),
        name="attn",
    )(*qkv)


def _mixout_kernel(x_ref, mod_ref, o_ref, xc_ref, halo_ref, bg_ref, sga_ref, sgc_ref,
                   cw_ref, wa_ref, wc_ref, wo_ref, out_ref):
    tm = x_ref.shape[0]
    i = pl.program_id(1)
    y_attn = _dot(o_ref[...], wa_ref[...])
    xc = xc_ref[...].astype(F32)
    halo = jnp.where(i > 0, halo_ref[...].astype(F32), 0.0)
    ext = jnp.concatenate([halo, xc], axis=0)
    pad = halo.shape[0]
    conv = xc * cw_ref[CONV_K - 1:CONV_K, :]
    for j in range(CONV_K - 1):
        back = CONV_K - 1 - j
        conv = conv + ext[pad - back:pad - back + tm, :] * cw_ref[j:j + 1, :]
    y_conv = _dot((bg_ref[...].astype(F32) * conv).astype(BF16), wc_ref[...])
    merged = (sga_ref[...].astype(F32) * y_attn + sgc_ref[...].astype(F32) * y_conv).astype(BF16)
    out_ref[...] = x_ref[...] + mod_ref[5:6, :] * _dot(merged, wo_ref[...])


def _mixout(x, mod, o, xc, bg, sga, sgc, conv_w, w_attn, w_conv, w_out, tm=512):
    b, s, d = x.shape
    halo_rows = 16
    row = lambda width: pl.BlockSpec((None, tm, width), lambda bi, i: (bi, i, 0))
    halo = pl.BlockSpec((None, halo_rows, d),
                        lambda bi, i: (bi, jnp.maximum(i * (tm // halo_rows) - 1, 0), 0))
    return pl.pallas_call(
        _mixout_kernel,
        out_shape=jax.ShapeDtypeStruct(x.shape, F32),
        grid=(b, s // tm),
        in_specs=[row(d), pl.BlockSpec((None, N_MOD, d), lambda bi, i: (bi, 0, 0)), row(o.shape[-1]),
                  row(d), halo, row(d), row(d), row(d),
                  _resident(conv_w.shape), _resident(w_attn.shape), _resident(w_conv.shape), _resident(w_out.shape)],
        out_specs=row(d),
        compiler_params=_params(2),
        name="mixout",
    )(x, mod, o, xc, xc, bg, sga, sgc, conv_w, w_attn, w_conv, w_out)


def kernel(x, c, w_ada, b_ada, norm_ffn1, ffn1_w_gate, ffn1_w_up, ffn1_w_down, norm_mix, w_in, q_norm, k_norm,
           conv_w, w_attn_branch, w_conv_branch, w_out, norm_ffn2, ffn2_w_gate, ffn2_w_up, ffn2_w_down):
    b, s, d = x.shape
    depth = w_ada.shape[0]
    c_pad = jnp.zeros((8, d), F32).at[:b].set(c)
    for l in range(depth):
        mod = _adaln(c_pad, w_ada[l], b_ada[l][None, :])[:b].reshape(b, N_MOD, d)
        x, w_in_b = _ffn(x, mod, norm_ffn1[l][None, :], ffn1_w_gate[l].astype(BF16), ffn1_w_up[l].astype(BF16),
                         ffn1_w_down[l].astype(BF16), mod_row=0, cast_ws=(w_in[l],))
        res = _proj(x, mod, norm_mix[l][None, :], q_norm[l][None, :], k_norm[l][None, :], w_in_b,
                    cast_ws=(w_attn_branch[l], w_conv_branch[l], w_out[l], ffn2_w_gate[l], ffn2_w_up[l], ffn2_w_down[l]))
        n_qkv = 3 * N_GROUPS
        o = _attn(res[:n_qkv])
        w_attn_b, w_conv_b, w_out_b, w_gate_b, w_up_b, w_down_b = res[n_qkv + 4:]
        x = _mixout(x, mod, o, *res[n_qkv:n_qkv + 4], conv_w[l], w_attn_b, w_conv_b, w_out_b)
        x, = _ffn(x, mod, norm_ffn2[l][None, :], w_gate_b, w_up_b, w_down_b, mod_row=6)
    return x
```

```python
import functools

import jax
import jax.numpy as jnp
from jax import lax
from jax.experimental import pallas as pl
from jax.experimental.pallas import tpu as pltpu

EPS = 1e-6
N_MOD = 9
N_GROUPS = 3
HEADS = 4
HEAD_DIM = 128
DILATIONS = (1, 4, 16)
BAND = 128
CHUNK = BAND * max(DILATIONS)
TILE = 512
FFN_SUB_ROWS = 256
CONV_K = 3
LOG2_E = 1.4426950408889634

V7X_VMEM_BYTES = 64 * 1024 * 1024
VMEM_LIMIT = V7X_VMEM_BYTES - 8 * 1024 * 1024

BF16 = jnp.bfloat16
BF16_SUBLANES = 16
F32 = jnp.float32


def _resident(shape):
    return pl.BlockSpec(shape, lambda *_: (0,) * len(shape), pipeline_mode=pl.Buffered(1))


def _params(n_axes):
    return pltpu.CompilerParams(dimension_semantics=("arbitrary",) * n_axes, vmem_limit_bytes=VMEM_LIMIT)


def _rms_mod(x, gain, shift, scale):
    ms = jnp.mean(x * x, axis=-1, keepdims=True)
    y = x * lax.rsqrt(ms + EPS) * gain
    return y * (1.0 + scale) + shift


def _dot(a, b):
    return jnp.dot(a, b, preferred_element_type=F32)


def _cast_specs(ws, grid):
    n_steps = 1
    for g in grid:
        n_steps *= g
    specs, shapes = [], []
    for w in ws:
        rows, cols = w.shape
        n_bands = n_steps
        while rows % n_bands or (rows // n_bands) % BF16_SUBLANES:
            n_bands //= 2
        per = n_steps // n_bands

        def index(*ids, per=per):
            flat = ids[0]
            for i, g in zip(ids[1:], grid[1:]):
                flat = flat * g + i
            return (flat // per, 0)

        specs.append(pl.BlockSpec((rows // n_bands, cols), index))
        shapes.append(jax.ShapeDtypeStruct(w.shape, BF16))
    return specs, shapes


def _cast_bands(src_refs, dst_refs):
    for src, dst in zip(src_refs, dst_refs):
        dst[...] = src[...].astype(BF16)


def _adaln_kernel(c_ref, w_ref, b_ref, o_ref):
    c = c_ref[...]
    c_act = c * jax.nn.sigmoid(c)
    o_ref[...] = _dot(c_act, w_ref[...]) + b_ref[...]


def _adaln(c_pad, w_ada, b_ada, tn=1152):
    rows, d = c_pad.shape
    n = w_ada.shape[1]
    return pl.pallas_call(
        _adaln_kernel,
        out_shape=jax.ShapeDtypeStruct((rows, n), F32),
        grid=(n // tn,),
        in_specs=[
            pl.BlockSpec((rows, d), lambda j: (0, 0)),
            pl.BlockSpec((d, tn), lambda j: (0, j)),
            pl.BlockSpec((1, tn), lambda j: (0, j)),
        ],
        out_specs=pl.BlockSpec((rows, tn), lambda j: (0, j)),
        compiler_params=_params(1),
        name="adaln",
    )(c_pad, w_ada, b_ada)


def _ffn_kernel(mod_row, n_cast, x_ref, mod_ref, g_ref, wg_ref, wu_ref, wd_ref, *rest):
    o_ref = rest[n_cast]
    _cast_bands(rest[:n_cast], rest[n_cast + 1:])
    shift = mod_ref[mod_row:mod_row + 1, :]
    scale = mod_ref[mod_row + 1:mod_row + 2, :]
    gate = mod_ref[mod_row + 2:mod_row + 3, :]
    for r0 in range(0, x_ref.shape[0], FFN_SUB_ROWS):
        x = x_ref[r0:r0 + FFN_SUB_ROWS, :]
        h = _rms_mod(x, g_ref[...], shift, scale).astype(BF16)
        a = _dot(h, wg_ref[...])
        u = _dot(h, wu_ref[...])
        act = (a * jax.nn.sigmoid(a) * u).astype(BF16)
        y = _dot(act, wd_ref[...])
        o_ref[r0:r0 + FFN_SUB_ROWS, :] = x + 0.5 * gate * y


def _ffn(x, mod, gain, w_gate, w_up, w_down, mod_row, cast_ws=(), tm=1024):
    b, s, d = x.shape
    f = w_gate.shape[1]
    grid = (b, s // tm)
    cast_specs, cast_shapes = _cast_specs(cast_ws, grid)
    row = pl.BlockSpec((None, tm, d), lambda bi, i: (bi, i, 0))
    return pl.pallas_call(
        functools.partial(_ffn_kernel, mod_row, len(cast_ws)),
        out_shape=[jax.ShapeDtypeStruct(x.shape, F32)] + cast_shapes,
        grid=grid,
        in_specs=[
            row,
            pl.BlockSpec((None, N_MOD, d), lambda bi, i: (bi, 0, 0)),
            _resident((1, d)),
            _resident((d, f)),
            _resident((d, f)),
            _resident((f, d)),
        ] + cast_specs,
        out_specs=[row] + cast_specs,
        compiler_params=_params(2),
        name="ffn",
    )(x, mod, gain, w_gate, w_up, w_down, *cast_ws)


def _regroup_rows(src, dst_f32, dst_bf16, seg):
    q = seg // 4
    for base in range(0, TILE, seg):
        for r in range(4):
            lo = base + r * q
            for j in range(src.shape[0]):
                piece = src[j, pl.ds(base + r, q, stride=4), :]
                if dst_f32 is not None:
                    dst_f32[j, lo:lo + q, :] = piece
                dst_bf16[lo:lo + q, j * HEAD_DIM:(j + 1) * HEAD_DIM] = piece.astype(BF16)


def _proj_kernel(n_cast, x_ref, mod_ref, g_ref, qn_ref, kn_ref, w_ref, *rest):
    cast_in, rest = rest[:n_cast], rest[n_cast:]
    qkv_refs = rest[:3 * N_GROUPS]
    xc_ref, bg_ref, sga_ref, sgc_ref = rest[3 * N_GROUPS:3 * N_GROUPS + 4]
    cast_out = rest[3 * N_GROUPS + 4:3 * N_GROUPS + 4 + n_cast]
    scr_a, scr_b, h_scr = rest[3 * N_GROUPS + 4 + n_cast:]
    _cast_bands(cast_in, cast_out)
    d = g_ref.shape[-1]
    aw = HEADS * HEAD_DIM
    hf = _rms_mod(x_ref[...], g_ref[...], mod_ref[3:4, :], mod_ref[4:5, :])
    h = hf.astype(BF16)
    for j in range(d // HEAD_DIM):
        scr_a[j] = hf[:, j * HEAD_DIM:(j + 1) * HEAD_DIM]
    _regroup_rows(scr_a, scr_b, h_scr.at[0], TILE)
    _regroup_rows(scr_b, None, h_scr.at[1], TILE // 4)
    lhs = (h, h_scr[0], h_scr[1])
    o = 3 * N_GROUPS * aw
    sga_ref[...] = jax.nn.sigmoid(_dot(h, w_ref[:, o + 3 * d:o + 4 * d])).astype(BF16)
    sgc_ref[...] = jax.nn.sigmoid(_dot(h, w_ref[:, o + 4 * d:o + 5 * d])).astype(BF16)
    u = _dot(h, w_ref[:, o:o + d])
    cg = _dot(h, w_ref[:, o + 2 * d:o + 3 * d])
    xc_ref[...] = (cg * u).astype(BF16)
    bg_ref[...] = _dot(h, w_ref[:, o + d:o + 2 * d]).astype(BF16)
    gains = (qn_ref[...] * (HEAD_DIM ** -0.5 * LOG2_E), kn_ref[...], None)
    for j in range(3):
        for g in range(N_GROUPS):
            c0 = (j * N_GROUPS + g) * aw
            t = _dot(lhs[g], w_ref[:, c0:c0 + aw])
            out_ref = qkv_refs[3 * g + j]
            for hd in range(HEADS):
                col = hd * HEAD_DIM
                slab = t[:, col:col + HEAD_DIM]
                if gains[j] is not None:
                    ms = jnp.mean(slab * slab, axis=-1, keepdims=True)
                    slab = slab * lax.rsqrt(ms + EPS) * gains[j]
                out_ref[:, col:col + HEAD_DIM] = slab.astype(BF16)


def _proj(x, mod, gain, q_norm, k_norm, w, cast_ws=()):
    b, s, d = x.shape
    aw = HEADS * HEAD_DIM
    grid = (b, s // TILE)
    cast_specs, cast_shapes = _cast_specs(cast_ws, grid)
    row = lambda width: pl.BlockSpec((None, TILE, width), lambda bi, i: (bi, i, 0))
    return pl.pallas_call(
        functools.partial(_proj_kernel, len(cast_ws)),
        out_shape=([jax.ShapeDtypeStruct((b, s, aw), BF16)] * (3 * N_GROUPS) + [jax.ShapeDtypeStruct((b, s, d), BF16)] * 4
                   + cast_shapes),
        grid=grid,
        in_specs=[
            row(d),
            pl.BlockSpec((None, N_MOD, d), lambda bi, i: (bi, 0, 0)),
            _resident((1, d)),
            _resident((1, HEAD_DIM)),
            _resident((1, HEAD_DIM)),
            _resident(w.shape),
        ] + cast_specs,
        out_specs=[row(aw)] * (3 * N_GROUPS) + [row(d)] * 4 + cast_specs,
        scratch_shapes=[pltpu.VMEM((d // HEAD_DIM, TILE, HEAD_DIM), F32), pltpu.VMEM((d // HEAD_DIM, TILE, HEAD_DIM), F32),
                        pltpu.VMEM((2, TILE, d), BF16)],
        compiler_params=_params(2),
        name="proj",
    )(x, mod, gain, q_norm, k_norm, w, *cast_ws)


def _attn_block(q, kcat, vcat, bias):
    s = lax.dot_general(q, kcat, (((1,), (1,)), ((), ())), preferred_element_type=F32) + bias
    m = jnp.max(s, axis=-1, keepdims=True)
    p = jnp.exp2(s - m)
    od = _dot(p.astype(BF16), jnp.concatenate([vcat, jnp.ones_like(vcat)], axis=1))
    den = od[:, HEAD_DIM:]
    return od[:, :HEAD_DIM] / den, m + jnp.log2(den)


def _band_rows(dil, r, jb):
    per_tile = TILE // dil
    if dil == 1:
        return [(jb * BAND, BAND)]
    pos = r if dil == 4 else (r % 4) * 4 + r // 4
    tiles = BAND // per_tile
    return [((jb * tiles + t) * TILE + pos * per_tile, per_tile) for t in range(tiles)]


def _gather(ref, ranges):
    parts = [ref[a:a + n, :] for a, n in ranges]
    return parts[0] if len(parts) == 1 else jnp.concatenate(parts, axis=0)


def _attn_kernel(*refs):
    ins = refs[:3 * N_GROUPS]
    out_ref = refs[3 * N_GROUPS]
    scr = refs[3 * N_GROUPS + 1:]
    kprev, vprev = scr[0:N_GROUPS], scr[N_GROUPS:2 * N_GROUPS]
    o_nat, l_nat = scr[2 * N_GROUPS:3 * N_GROUPS], scr[3 * N_GROUPS:4 * N_GROUPS]
    c = pl.program_id(2)

    @pl.when(c == 0)
    def _():
        for ref in kprev + vprev:
            ref[...] = jnp.zeros(ref.shape, ref.dtype)

    qi = lax.broadcasted_iota(jnp.int32, (BAND, 2 * BAND), 0)
    kj = lax.broadcasted_iota(jnp.int32, (BAND, 2 * BAND), 1)
    in_prev = (kj < BAND) & (kj >= qi)
    in_cur = (kj >= BAND) & (kj - BAND <= qi)
    neg = jnp.float32(-jnp.inf)
    bias_in = jnp.where(in_prev | in_cur, 0.0, neg)
    bias_first = jnp.where(in_cur, 0.0, neg)
    bias_edge = jnp.where(c > 0, bias_in, bias_first)

    for g, dil in enumerate(DILATIONS):
        q_ref, k_ref, v_ref = ins[3 * g:3 * g + 3]
        blocks = CHUNK // dil // BAND
        carry = [rg for r in range(dil) for rg in _band_rows(dil, r, blocks - 1)]
        for r in range(dil):
            for jb in range(blocks):
                cur = _band_rows(dil, r, jb)
                q = _gather(q_ref, cur)
                if jb == 0:
                    base = r * BAND
                    k_prev = kprev[g][base:base + BAND, :]
                    v_prev = vprev[g][base:base + BAND, :]
                    bias = bias_edge
                else:
                    prev = _band_rows(dil, r, jb - 1)
                    k_prev, v_prev = _gather(k_ref, prev), _gather(v_ref, prev)
                    bias = bias_in
                kcat = jnp.concatenate([k_prev, _gather(k_ref, cur)], axis=0)
                vcat = jnp.concatenate([v_prev, _gather(v_ref, cur)], axis=0)
                o, lse = _attn_block(q, kcat, vcat, bias)
                if dil == 1:
                    rows = pl.ds(jb * BAND, BAND)
                else:
                    rows = pl.ds(jb * BAND * dil + r, BAND, stride=dil)
                o_nat[g][rows, :] = o
                l_nat[g][rows, :] = lse
        off = 0
        for a, n in carry:
            kprev[g][off:off + n, :] = k_ref[a:a + n, :]
            vprev[g][off:off + n, :] = v_ref[a:a + n, :]
            off += n

    step = 256
    for a in range(0, CHUNK, step):
        ls = [l_nat[g][a:a + step, :] for g in range(N_GROUPS)]
        m = jnp.maximum(jnp.maximum(ls[0], ls[1]), ls[2])
        es = [jnp.exp2(l - m) for l in ls]
        num = es[0] * o_nat[0][a:a + step, :] + es[1] * o_nat[1][a:a + step, :] + es[2] * o_nat[2][a:a + step, :]
        out_ref[a:a + step, :] = (num / (es[0] + es[1] + es[2])).astype(BF16)


def _attn(qkv):
    b, s, aw = qkv[0].shape
    blk = pl.BlockSpec((None, CHUNK, HEAD_DIM), lambda bi, h, c: (bi, c, h))
    carry_rows = [BAND * dil for dil in DILATIONS]
    return pl.pallas_call(
        _attn_kernel,
        out_shape=jax.ShapeDtypeStruct((b, s, aw), BF16),
        grid=(b, HEADS, s // CHUNK),
        in_specs=[blk] * len(qkv),
        out_specs=blk,
        scratch_shapes=(
            [pltpu.VMEM((n, HEAD_DIM), BF16) for n in carry_rows] * 2
            + [pltpu.VMEM((CHUNK, HEAD_DIM), F32)] * (2 * N_GROUPS)),
        compiler_params=_params(3),
        name="attn",
    )(*qkv)


def _mixout_kernel(x_ref, mod_ref, o_ref, xc_ref, halo_ref, bg_ref, sga_ref, sgc_ref,
                   cw_ref, wa_ref, wc_ref, wo_ref, out_ref):
    tm = x_ref.shape[0]
    i = pl.program_id(1)
    y_attn = _dot(o_ref[...], wa_ref[...])
    xc = xc_ref[...].astype(F32)
    halo = jnp.where(i > 0, halo_ref[...].astype(F32), 0.0)
    ext = jnp.concatenate([halo, xc], axis=0)
    pad = halo.shape[0]
    conv = xc * cw_ref[CONV_K - 1:CONV_K, :]
    for j in range(CONV_K - 1):
        back = CONV_K - 1 - j
        conv = conv + ext[pad - back:pad - back + tm, :] * cw_ref[j:j + 1, :]
    y_conv = _dot((bg_ref[...].astype(F32) * conv).astype(BF16), wc_ref[...])
    merged = (sga_ref[...].astype(F32) * y_attn + sgc_ref[...].astype(F32) * y_conv).astype(BF16)
    out_ref[...] = x_ref[...] + mod_ref[5:6, :] * _dot(merged, wo_ref[...])


def _mixout(x, mod, o, xc, bg, sga, sgc, conv_w, w_attn, w_conv, w_out, tm=512):
    b, s, d = x.shape
    halo_rows = 16
    row = lambda width: pl.BlockSpec((None, tm, width), lambda bi, i: (bi, i, 0))
    halo = pl.BlockSpec((None, halo_rows, d),
                        lambda bi, i: (bi, jnp.maximum(i * (tm // halo_rows) - 1, 0), 0))
    return pl.pallas_call(
        _mixout_kernel,
        out_shape=jax.ShapeDtypeStruct(x.shape, F32),
        grid=(b, s // tm),
        in_specs=[row(d), pl.BlockSpec((None, N_MOD, d), lambda bi, i: (bi, 0, 0)), row(o.shape[-1]),
                  row(d), halo, row(d), row(d), row(d),
                  _resident(conv_w.shape), _resident(w_attn.shape), _resident(w_conv.shape), _resident(w_out.shape)],
        out_specs=row(d),
        compiler_params=_params(2),
        name="mixout",
    )(x, mod, o, xc, xc, bg, sga, sgc, conv_w, w_attn, w_conv, w_out)


def kernel(x, c, w_ada, b_ada, norm_ffn1, ffn1_w_gate, ffn1_w_up, ffn1_w_down, norm_mix, w_in, q_norm, k_norm,
           conv_w, w_attn_branch, w_conv_branch, w_out, norm_ffn2, ffn2_w_gate, ffn2_w_up, ffn2_w_down):
    b, s, d = x.shape
    depth = w_ada.shape[0]
    c_pad = jnp.zeros((8, d), F32).at[:b].set(c)
    for l in range(depth):
        mod = _adaln(c_pad, w_ada[l], b_ada[l][None, :])[:b].reshape(b, N_MOD, d)
        x, w_in_b = _ffn(x, mod, norm_ffn1[l][None, :], ffn1_w_gate[l].astype(BF16), ffn1_w_up[l].astype(BF16),
                         ffn1_w_down[l].astype(BF16), mod_row=0, cast_ws=(w_in[l],))
        res = _proj(x, mod, norm_mix[l][None, :], q_norm[l][None, :], k_norm[l][None, :], w_in_b,
                    cast_ws=(w_attn_branch[l], w_conv_branch[l], w_out[l], ffn2_w_gate[l], ffn2_w_up[l], ffn2_w_down[l]))
        n_qkv = 3 * N_GROUPS
        o = _attn(res[:n_qkv])
        w_attn_b, w_conv_b, w_out_b, w_gate_b, w_up_b, w_down_b = res[n_qkv + 4:]
        x = _mixout(x, mod, o, *res[n_qkv:n_qkv + 4], conv_w[l], w_attn_b, w_conv_b, w_out_b)
        x, = _ffn(x, mod, norm_ffn2[l][None, :], w_gate_b, w_up_b, w_down_b, mod_row=6)
    return x
```

```python
import functools

import jax
import jax.numpy as jnp
from jax import lax
from jax.experimental import pallas as pl
from jax.experimental.pallas import tpu as pltpu

EPS = 1e-6
N_MOD = 9
N_GROUPS = 3
HEADS = 4
HEAD_DIM = 128
DILATIONS = (1, 4, 16)
BAND = 128
CHUNK = BAND * max(DILATIONS)
TILE = 512
FFN_SUB_ROWS = 256
MIX_SUB_ROWS = 256
CONV_K = 3
LOG2_E = 1.4426950408889634

V7X_VMEM_BYTES = 64 * 1024 * 1024
VMEM_LIMIT = V7X_VMEM_BYTES - 8 * 1024 * 1024

BF16 = jnp.bfloat16
BF16_SUBLANES = 16
F32 = jnp.float32


def _resident(shape):
    return pl.BlockSpec(shape, lambda *_: (0,) * len(shape), pipeline_mode=pl.Buffered(1))


def _params(n_axes):
    return pltpu.CompilerParams(dimension_semantics=("arbitrary",) * n_axes, vmem_limit_bytes=VMEM_LIMIT)


def _rms_mod(x, gain, shift, scale):
    ms = jnp.mean(x * x, axis=-1, keepdims=True)
    y = x * lax.rsqrt(ms + EPS) * gain
    return y * (1.0 + scale) + shift


def _dot(a, b):
    return jnp.dot(a, b, preferred_element_type=F32)


def _cast_specs(ws, grid):
    n_steps = 1
    for g in grid:
        n_steps *= g
    specs, shapes = [], []
    for w in ws:
        rows, cols = w.shape
        n_bands = n_steps
        while rows % n_bands or (rows // n_bands) % BF16_SUBLANES:
            n_bands //= 2
        per = n_steps // n_bands

        def index(*ids, per=per):
            flat = ids[0]
            for i, g in zip(ids[1:], grid[1:]):
                flat = flat * g + i
            return (flat // per, 0)

        specs.append(pl.BlockSpec((rows // n_bands, cols), index))
        shapes.append(jax.ShapeDtypeStruct(w.shape, BF16))
    return specs, shapes


def _cast_bands(src_refs, dst_refs):
    for src, dst in zip(src_refs, dst_refs):
        dst[...] = src[...].astype(BF16)


def _adaln_kernel(n_cast, c_ref, w_ref, b_ref, *rest):
    o_ref = rest[n_cast]
    _cast_bands(rest[:n_cast], rest[n_cast + 1:])
    c = c_ref[...]
    c_act = c * jax.nn.sigmoid(c)
    o_ref[...] = _dot(c_act, w_ref[...]) + b_ref[...]


def _adaln(c_pad, w_ada, b_ada, cast_ws=(), tn=1152):
    rows, d = c_pad.shape
    n = w_ada.shape[1]
    grid = (n // tn,)
    cast_specs, cast_shapes = _cast_specs(cast_ws, grid)
    return pl.pallas_call(
        functools.partial(_adaln_kernel, len(cast_ws)),
        out_shape=[jax.ShapeDtypeStruct((rows, n), F32)] + cast_shapes,
        grid=grid,
        in_specs=[
            pl.BlockSpec((rows, d), lambda j: (0, 0)),
            pl.BlockSpec((d, tn), lambda j: (0, j)),
            pl.BlockSpec((1, tn), lambda j: (0, j)),
        ] + cast_specs,
        out_specs=[pl.BlockSpec((rows, tn), lambda j: (0, j))] + cast_specs,
        compiler_params=_params(1),
        name="adaln",
    )(c_pad, w_ada, b_ada, *cast_ws)


def _ffn_kernel(mod_row, n_cast, x_ref, mod_ref, g_ref, wg_ref, wu_ref, wd_ref, *rest):
    o_ref = rest[n_cast]
    _cast_bands(rest[:n_cast], rest[n_cast + 1:])
    shift = mod_ref[mod_row:mod_row + 1, :]
    scale = mod_ref[mod_row + 1:mod_row + 2, :]
    gate = mod_ref[mod_row + 2:mod_row + 3, :]
    for r0 in range(0, x_ref.shape[0], FFN_SUB_ROWS):
        x = x_ref[r0:r0 + FFN_SUB_ROWS, :]
        h = _rms_mod(x, g_ref[...], shift, scale).astype(BF16)
        a = _dot(h, wg_ref[...])
        u = _dot(h, wu_ref[...])
        act = (a * jax.nn.sigmoid(a) * u).astype(BF16)
        y = _dot(act, wd_ref[...])
        o_ref[r0:r0 + FFN_SUB_ROWS, :] = x + 0.5 * gate * y


def _ffn(x, mod, gain, w_gate, w_up, w_down, mod_row, cast_ws=(), tm=1024):
    b, s, d = x.shape
    f = w_gate.shape[1]
    grid = (b, s // tm)
    cast_specs, cast_shapes = _cast_specs(cast_ws, grid)
    row = pl.BlockSpec((None, tm, d), lambda bi, i: (bi, i, 0))
    return pl.pallas_call(
        functools.partial(_ffn_kernel, mod_row, len(cast_ws)),
        out_shape=[jax.ShapeDtypeStruct(x.shape, F32)] + cast_shapes,
        grid=grid,
        in_specs=[
            row,
            pl.BlockSpec((None, N_MOD, d), lambda bi, i: (bi, 0, 0)),
            _resident((1, d)),
            _resident((d, f)),
            _resident((d, f)),
            _resident((f, d)),
        ] + cast_specs,
        out_specs=[row] + cast_specs,
        compiler_params=_params(2),
        name="ffn",
    )(x, mod, gain, w_gate, w_up, w_down, *cast_ws)


def _regroup_rows(src, dst_f32, dst_bf16, seg):
    q = seg // 4
    for base in range(0, TILE, seg):
        for r in range(4):
            lo = base + r * q
            for j in range(src.shape[0]):
                piece = src[j, pl.ds(base + r, q, stride=4), :]
                if dst_f32 is not None:
                    dst_f32[j, lo:lo + q, :] = piece
                dst_bf16[lo:lo + q, j * HEAD_DIM:(j + 1) * HEAD_DIM] = piece.astype(BF16)


def _proj_kernel(n_cast, x_ref, mod_ref, g_ref, qn_ref, kn_ref, w_ref, *rest):
    cast_in, rest = rest[:n_cast], rest[n_cast:]
    qkv_refs = rest[:3 * N_GROUPS]
    xc_ref, bg_ref, sga_ref, sgc_ref = rest[3 * N_GROUPS:3 * N_GROUPS + 4]
    cast_out = rest[3 * N_GROUPS + 4:3 * N_GROUPS + 4 + n_cast]
    scr_a, scr_b, h_scr = rest[3 * N_GROUPS + 4 + n_cast:]
    _cast_bands(cast_in, cast_out)
    d = g_ref.shape[-1]
    aw = HEADS * HEAD_DIM
    hf = _rms_mod(x_ref[...], g_ref[...], mod_ref[3:4, :], mod_ref[4:5, :])
    h = hf.astype(BF16)
    for j in range(d // HEAD_DIM):
        scr_a[j] = hf[:, j * HEAD_DIM:(j + 1) * HEAD_DIM]
    _regroup_rows(scr_a, scr_b, h_scr.at[0], TILE)
    _regroup_rows(scr_b, None, h_scr.at[1], TILE // 4)
    lhs = (h, h_scr[0], h_scr[1])
    o = 3 * N_GROUPS * aw
    sga_ref[...] = jax.nn.sigmoid(_dot(h, w_ref[:, o + 3 * d:o + 4 * d])).astype(BF16)
    sgc_ref[...] = jax.nn.sigmoid(_dot(h, w_ref[:, o + 4 * d:o + 5 * d])).astype(BF16)
    u = _dot(h, w_ref[:, o:o + d])
    cg = _dot(h, w_ref[:, o + 2 * d:o + 3 * d])
    xc_ref[...] = (cg * u).astype(BF16)
    bg_ref[...] = _dot(h, w_ref[:, o + d:o + 2 * d]).astype(BF16)
    gains = (qn_ref[...] * (HEAD_DIM ** -0.5 * LOG2_E), kn_ref[...], None)
    for j in range(3):
        for g in range(N_GROUPS):
            c0 = (j * N_GROUPS + g) * aw
            t = _dot(lhs[g], w_ref[:, c0:c0 + aw])
            out_ref = qkv_refs[3 * g + j]
            for hd in range(HEADS):
                col = hd * HEAD_DIM
                slab = t[:, col:col + HEAD_DIM]
                if gains[j] is not None:
                    ms = jnp.mean(slab * slab, axis=-1, keepdims=True)
                    slab = slab * lax.rsqrt(ms + EPS) * gains[j]
                out_ref[:, col:col + HEAD_DIM] = slab.astype(BF16)


def _proj(x, mod, gain, q_norm, k_norm, w, cast_ws=()):
    b, s, d = x.shape
    aw = HEADS * HEAD_DIM
    grid = (b, s // TILE)
    cast_specs, cast_shapes = _cast_specs(cast_ws, grid)
    row = lambda width: pl.BlockSpec((None, TILE, width), lambda bi, i: (bi, i, 0))
    return pl.pallas_call(
        functools.partial(_proj_kernel, len(cast_ws)),
        out_shape=([jax.ShapeDtypeStruct((b, s, aw), BF16)] * (3 * N_GROUPS) + [jax.ShapeDtypeStruct((b, s, d), BF16)] * 4
                   + cast_shapes),
        grid=grid,
        in_specs=[
            row(d),
            pl.BlockSpec((None, N_MOD, d), lambda bi, i: (bi, 0, 0)),
            _resident((1, d)),
            _resident((1, HEAD_DIM)),
            _resident((1, HEAD_DIM)),
            _resident(w.shape),
        ] + cast_specs,
        out_specs=[row(aw)] * (3 * N_GROUPS) + [row(d)] * 4 + cast_specs,
        scratch_shapes=[pltpu.VMEM((d // HEAD_DIM, TILE, HEAD_DIM), F32), pltpu.VMEM((d // HEAD_DIM, TILE, HEAD_DIM), F32),
                        pltpu.VMEM((2, TILE, d), BF16)],
        compiler_params=_params(2),
        name="proj",
    )(x, mod, gain, q_norm, k_norm, w, *cast_ws)


def _attn_block(q, kcat, vcat, bias):
    s = lax.dot_general(q, kcat, (((1,), (1,)), ((), ())), preferred_element_type=F32) + bias
    m = jnp.max(s, axis=-1, keepdims=True)
    p = jnp.exp2(s - m)
    od = _dot(p.astype(BF16), jnp.concatenate([vcat, jnp.ones_like(vcat)], axis=1))
    den = od[:, HEAD_DIM:]
    return od[:, :HEAD_DIM] / den, m + jnp.log2(den)


def _band_rows(dil, r, jb):
    per_tile = TILE // dil
    if dil == 1:
        return [(jb * BAND, BAND)]
    pos = r if dil == 4 else (r % 4) * 4 + r // 4
    tiles = BAND // per_tile
    return [((jb * tiles + t) * TILE + pos * per_tile, per_tile) for t in range(tiles)]


def _gather(ref, ranges):
    parts = [ref[a:a + n, :] for a, n in ranges]
    return parts[0] if len(parts) == 1 else jnp.concatenate(parts, axis=0)


def _attn_kernel(*refs):
    ins = refs[:3 * N_GROUPS]
    out_ref = refs[3 * N_GROUPS]
    scr = refs[3 * N_GROUPS + 1:]
    kprev, vprev = scr[0:N_GROUPS], scr[N_GROUPS:2 * N_GROUPS]
    o_nat, l_nat = scr[2 * N_GROUPS:3 * N_GROUPS], scr[3 * N_GROUPS:4 * N_GROUPS]
    c = pl.program_id(2)

    @pl.when(c == 0)
    def _():
        for ref in kprev + vprev:
            ref[...] = jnp.zeros(ref.shape, ref.dtype)

    qi = lax.broadcasted_iota(jnp.int32, (BAND, 2 * BAND), 0)
    kj = lax.broadcasted_iota(jnp.int32, (BAND, 2 * BAND), 1)
    in_prev = (kj < BAND) & (kj >= qi)
    in_cur = (kj >= BAND) & (kj - BAND <= qi)
    neg = jnp.float32(-jnp.inf)
    bias_in = jnp.where(in_prev | in_cur, 0.0, neg)
    bias_first = jnp.where(in_cur, 0.0, neg)
    bias_edge = jnp.where(c > 0, bias_in, bias_first)

    for g, dil in enumerate(DILATIONS):
        q_ref, k_ref, v_ref = ins[3 * g:3 * g + 3]
        blocks = CHUNK // dil // BAND
        carry = [rg for r in range(dil) for rg in _band_rows(dil, r, blocks - 1)]
        for r in range(dil):
            for jb in range(blocks):
                cur = _band_rows(dil, r, jb)
                q = _gather(q_ref, cur)
                if jb == 0:
                    base = r * BAND
                    k_prev = kprev[g][base:base + BAND, :]
                    v_prev = vprev[g][base:base + BAND, :]
                    bias = bias_edge
                else:
                    prev = _band_rows(dil, r, jb - 1)
                    k_prev, v_prev = _gather(k_ref, prev), _gather(v_ref, prev)
                    bias = bias_in
                kcat = jnp.concatenate([k_prev, _gather(k_ref, cur)], axis=0)
                vcat = jnp.concatenate([v_prev, _gather(v_ref, cur)], axis=0)
                o, lse = _attn_block(q, kcat, vcat, bias)
                if dil == 1:
                    rows = pl.ds(jb * BAND, BAND)
                else:
                    rows = pl.ds(jb * BAND * dil + r, BAND, stride=dil)
                o_nat[g][rows, :] = o
                l_nat[g][rows, :] = lse
        off = 0
        for a, n in carry:
            kprev[g][off:off + n, :] = k_ref[a:a + n, :]
            vprev[g][off:off + n, :] = v_ref[a:a + n, :]
            off += n

    step = 256
    for a in range(0, CHUNK, step):
        ls = [l_nat[g][a:a + step, :] for g in range(N_GROUPS)]
        m = jnp.maximum(jnp.maximum(ls[0], ls[1]), ls[2])
        es = [jnp.exp2(l - m) for l in ls]
        num = es[0] * o_nat[0][a:a + step, :] + es[1] * o_nat[1][a:a + step, :] + es[2] * o_nat[2][a:a + step, :]
        out_ref[a:a + step, :] = (num / (es[0] + es[1] + es[2])).astype(BF16)


def _attn(qkv):
    b, s, aw = qkv[0].shape
    blk = pl.BlockSpec((None, CHUNK, HEAD_DIM), lambda bi, h, c: (bi, c, h))
    carry_rows = [BAND * dil for dil in DILATIONS]
    return pl.pallas_call(
        _attn_kernel,
        out_shape=jax.ShapeDtypeStruct((b, s, aw), BF16),
        grid=(b, HEADS, s // CHUNK),
        in_specs=[blk] * len(qkv),
        out_specs=blk,
        scratch_shapes=(
            [pltpu.VMEM((n, HEAD_DIM), BF16) for n in carry_rows] * 2
            + [pltpu.VMEM((CHUNK, HEAD_DIM), F32)] * (2 * N_GROUPS)),
        compiler_params=_params(3),
        name="attn",
    )(*qkv)


def _mixout_kernel(x_ref, mod_ref, o_ref, xc_ref, halo_ref, bg_ref, sga_ref, sgc_ref,
                   cw_ref, wa_ref, wc_ref, wo_ref, out_ref):
    i = pl.program_id(1)
    pad = halo_ref.shape[0]
    for r0 in range(0, x_ref.shape[0], MIX_SUB_ROWS):
        rows = slice(r0, r0 + MIX_SUB_ROWS)
        y_attn = _dot(o_ref[rows, :], wa_ref[...])
        xc = xc_ref[rows, :].astype(F32)
        if r0 == 0:
            before = jnp.where(i > 0, halo_ref[...].astype(F32), 0.0)
        else:
            before = xc_ref[r0 - pad:r0, :].astype(F32)
        ext = jnp.concatenate([before, xc], axis=0)
        conv = xc * cw_ref[CONV_K - 1:CONV_K, :]
        for j in range(CONV_K - 1):
            back = CONV_K - 1 - j
            conv = conv + ext[pad - back:pad - back + MIX_SUB_ROWS, :] * cw_ref[j:j + 1, :]
        y_conv = _dot((bg_ref[rows, :].astype(F32) * conv).astype(BF16), wc_ref[...])
        merged = (sga_ref[rows, :].astype(F32) * y_attn + sgc_ref[rows, :].astype(F32) * y_conv).astype(BF16)
        out_ref[rows, :] = x_ref[rows, :] + mod_ref[5:6, :] * _dot(merged, wo_ref[...])


def _mixout(x, mod, o, xc, bg, sga, sgc, conv_w, w_attn, w_conv, w_out, tm=1024):
    b, s, d = x.shape
    halo_rows = BF16_SUBLANES
    row = lambda width: pl.BlockSpec((None, tm, width), lambda bi, i: (bi, i, 0))
    halo = pl.BlockSpec((None, halo_rows, d),
                        lambda bi, i: (bi, jnp.maximum(i * (tm // halo_rows) - 1, 0), 0))
    return pl.pallas_call(
        _mixout_kernel,
        out_shape=jax.ShapeDtypeStruct(x.shape, F32),
        grid=(b, s // tm),
        in_specs=[row(d), pl.BlockSpec((None, N_MOD, d), lambda bi, i: (bi, 0, 0)), row(o.shape[-1]),
                  row(d), halo, row(d), row(d), row(d),
                  _resident(conv_w.shape), _resident(w_attn.shape), _resident(w_conv.shape), _resident(w_out.shape)],
        out_specs=row(d),
        compiler_params=_params(2),
        name="mixout",
    )(x, mod, o, xc, xc, bg, sga, sgc, conv_w, w_attn, w_conv, w_out)


def kernel(x, c, w_ada, b_ada, norm_ffn1, ffn1_w_gate, ffn1_w_up, ffn1_w_down, norm_mix, w_in, q_norm, k_norm,
           conv_w, w_attn_branch, w_conv_branch, w_out, norm_ffn2, ffn2_w_gate, ffn2_w_up, ffn2_w_down):
    b, s, d = x.shape
    depth = w_ada.shape[0]
    c_pad = jnp.zeros((8, d), F32).at[:b].set(c)
    for l in range(depth):
        mod, w_gate1, w_up1, w_down1 = _adaln(c_pad, w_ada[l], b_ada[l][None, :],
                                              cast_ws=(ffn1_w_gate[l], ffn1_w_up[l], ffn1_w_down[l]))
        mod = mod[:b].reshape(b, N_MOD, d)
        x, w_in_b = _ffn(x, mod, norm_ffn1[l][None, :], w_gate1, w_up1, w_down1, mod_row=0, cast_ws=(w_in[l],))
        res = _proj(x, mod, norm_mix[l][None, :], q_norm[l][None, :], k_norm[l][None, :], w_in_b,
                    cast_ws=(w_attn_branch[l], w_conv_branch[l], w_out[l], ffn2_w_gate[l], ffn2_w_up[l], ffn2_w_down[l]))
        n_qkv = 3 * N_GROUPS
        o = _attn(res[:n_qkv])
        w_attn_b, w_conv_b, w_out_b, w_gate_b, w_up_b, w_down_b = res[n_qkv + 4:]
        x = _mixout(x, mod, o, *res[n_qkv:n_qkv + 4], conv_w[l], w_attn_b, w_conv_b, w_out_b)
        x, = _ffn(x, mod, norm_ffn2[l][None, :], w_gate_b, w_up_b, w_down_b, mod_row=6)
    return x
```

```python
import functools

import jax
import jax.numpy as jnp
from jax import lax
from jax.experimental import pallas as pl
from jax.experimental.pallas import tpu as pltpu

EPS = 1e-6
N_MOD = 9
N_GROUPS = 3
HEADS = 4
HEAD_DIM = 128
DILATIONS = (1, 4, 16)
BAND = 128
CHUNK = BAND * max(DILATIONS)
TILE = 512
FFN_SUB_ROWS = 256
MIX_SUB_ROWS = 256
CONV_K = 3
LOG2_E = 1.4426950408889634

V7X_VMEM_BYTES = 64 * 1024 * 1024
VMEM_LIMIT = V7X_VMEM_BYTES - 8 * 1024 * 1024

BF16 = jnp.bfloat16
BF16_SUBLANES = 16
F32 = jnp.float32


def _resident(shape):
    return pl.BlockSpec(shape, lambda *_: (0,) * len(shape), pipeline_mode=pl.Buffered(1))


def _params(n_axes):
    return pltpu.CompilerParams(dimension_semantics=("arbitrary",) * n_axes, vmem_limit_bytes=VMEM_LIMIT)


def _rms_mod(x, gain, shift, scale):
    ms = jnp.mean(x * x, axis=-1, keepdims=True)
    return x * lax.rsqrt(ms + EPS) * (gain * (1.0 + scale)) + shift


def _dot(a, b):
    return jnp.dot(a, b, preferred_element_type=F32)


def _cast_specs(ws, grid):
    n_steps = 1
    for g in grid:
        n_steps *= g
    specs, shapes = [], []
    for w in ws:
        rows, cols = w.shape
        n_bands = n_steps
        while rows % n_bands or (rows // n_bands) % BF16_SUBLANES:
            n_bands //= 2
        per = n_steps // n_bands

        def index(*ids, per=per):
            flat = ids[0]
            for i, g in zip(ids[1:], grid[1:]):
                flat = flat * g + i
            return (flat // per, 0)

        specs.append(pl.BlockSpec((rows // n_bands, cols), index))
        shapes.append(jax.ShapeDtypeStruct(w.shape, BF16))
    return specs, shapes


def _cast_bands(src_refs, dst_refs):
    for src, dst in zip(src_refs, dst_refs):
        dst[...] = src[...].astype(BF16)


def _adaln_kernel(n_cast, c_ref, w_ref, b_ref, *rest):
    o_ref = rest[n_cast]
    _cast_bands(rest[:n_cast], rest[n_cast + 1:])
    c = c_ref[...]
    c_act = c * jax.nn.sigmoid(c)
    o_ref[...] = _dot(c_act, w_ref[...]) + b_ref[...]


def _adaln(c_pad, w_ada, b_ada, cast_ws=(), tn=1152):
    rows, d = c_pad.shape
    n = w_ada.shape[1]
    grid = (n // tn,)
    cast_specs, cast_shapes = _cast_specs(cast_ws, grid)
    return pl.pallas_call(
        functools.partial(_adaln_kernel, len(cast_ws)),
        out_shape=[jax.ShapeDtypeStruct((rows, n), F32)] + cast_shapes,
        grid=grid,
        in_specs=[
            pl.BlockSpec((rows, d), lambda j: (0, 0)),
            pl.BlockSpec((d, tn), lambda j: (0, j)),
            pl.BlockSpec((1, tn), lambda j: (0, j)),
        ] + cast_specs,
        out_specs=[pl.BlockSpec((rows, tn), lambda j: (0, j))] + cast_specs,
        compiler_params=_params(1),
        name="adaln",
    )(c_pad, w_ada, b_ada, *cast_ws)


def _ffn_kernel(mod_row, n_cast, x_ref, mod_ref, g_ref, wg_ref, wu_ref, wd_ref, *rest):
    o_ref = rest[n_cast]
    _cast_bands(rest[:n_cast], rest[n_cast + 1:])
    shift = mod_ref[mod_row:mod_row + 1, :]
    scale = mod_ref[mod_row + 1:mod_row + 2, :]
    gate = mod_ref[mod_row + 2:mod_row + 3, :]
    for r0 in range(0, x_ref.shape[0], FFN_SUB_ROWS):
        x = x_ref[r0:r0 + FFN_SUB_ROWS, :]
        h = _rms_mod(x, g_ref[...], shift, scale).astype(BF16)
        a = _dot(h, wg_ref[...])
        u = _dot(h, wu_ref[...])
        act = (a * jax.nn.sigmoid(a) * u).astype(BF16)
        y = _dot(act, wd_ref[...])
        o_ref[r0:r0 + FFN_SUB_ROWS, :] = x + 0.5 * gate * y


def _ffn(x, mod, gain, w_gate, w_up, w_down, mod_row, cast_ws=(), tm=1024):
    b, s, d = x.shape
    f = w_gate.shape[1]
    grid = (b, s // tm)
    cast_specs, cast_shapes = _cast_specs(cast_ws, grid)
    row = pl.BlockSpec((None, tm, d), lambda bi, i: (bi, i, 0))
    return pl.pallas_call(
        functools.partial(_ffn_kernel, mod_row, len(cast_ws)),
        out_shape=[jax.ShapeDtypeStruct(x.shape, F32)] + cast_shapes,
        grid=grid,
        in_specs=[
            row,
            pl.BlockSpec((None, N_MOD, d), lambda bi, i: (bi, 0, 0)),
            _resident((1, d)),
            _resident((d, f)),
            _resident((d, f)),
            _resident((f, d)),
        ] + cast_specs,
        out_specs=[row] + cast_specs,
        compiler_params=_params(2),
        name="ffn",
    )(x, mod, gain, w_gate, w_up, w_down, *cast_ws)


def _regroup_rows(src, dst_f32, dst_bf16, seg):
    q = seg // 4
    for base in range(0, TILE, seg):
        for r in range(4):
            lo = base + r * q
            for j in range(src.shape[0]):
                piece = src[j, pl.ds(base + r, q, stride=4), :]
                if dst_f32 is not None:
                    dst_f32[j, lo:lo + q, :] = piece
                dst_bf16[lo:lo + q, j * HEAD_DIM:(j + 1) * HEAD_DIM] = piece.astype(BF16)


def _proj_kernel(n_cast, x_ref, mod_ref, g_ref, qn_ref, kn_ref, w_ref, *rest):
    cast_in, rest = rest[:n_cast], rest[n_cast:]
    qkv_refs = rest[:3 * N_GROUPS]
    xc_ref, bg_ref, sga_ref, sgc_ref = rest[3 * N_GROUPS:3 * N_GROUPS + 4]
    cast_out = rest[3 * N_GROUPS + 4:3 * N_GROUPS + 4 + n_cast]
    scr_a, scr_b, h_scr = rest[3 * N_GROUPS + 4 + n_cast:]
    _cast_bands(cast_in, cast_out)
    d = g_ref.shape[-1]
    aw = HEADS * HEAD_DIM
    hf = _rms_mod(x_ref[...], g_ref[...], mod_ref[3:4, :], mod_ref[4:5, :])
    h = hf.astype(BF16)
    for j in range(d // HEAD_DIM):
        scr_a[j] = hf[:, j * HEAD_DIM:(j + 1) * HEAD_DIM]
    _regroup_rows(scr_a, scr_b, h_scr.at[0], TILE)
    _regroup_rows(scr_b, None, h_scr.at[1], TILE // 4)
    o = 3 * N_GROUPS * aw
    sga_ref[...] = jax.nn.sigmoid(_dot(h, w_ref[:, o + 3 * d:o + 4 * d])).astype(BF16)
    sgc_ref[...] = jax.nn.sigmoid(_dot(h, w_ref[:, o + 4 * d:o + 5 * d])).astype(BF16)
    u = _dot(h, w_ref[:, o:o + d])
    cg = _dot(h, w_ref[:, o + 2 * d:o + 3 * d])
    xc_ref[...] = (cg * u).astype(BF16)
    bg_ref[...] = _dot(h, w_ref[:, o + d:o + 2 * d]).astype(BF16)
    gains = (qn_ref[...] * (HEAD_DIM ** -0.5 * LOG2_E), kn_ref[...], None)
    for j in range(3):
        for g in range(N_GROUPS):
            c0 = (j * N_GROUPS + g) * aw
            t = _dot(h if g == 0 else h_scr[g - 1], w_ref[:, c0:c0 + aw])
            out_ref = qkv_refs[3 * g + j]
            for hd in range(HEADS):
                col = hd * HEAD_DIM
                slab = t[:, col:col + HEAD_DIM]
                if gains[j] is not None:
                    ms = jnp.mean(slab * slab, axis=-1, keepdims=True)
                    slab = slab * lax.rsqrt(ms + EPS) * gains[j]
                out_ref[:, col:col + HEAD_DIM] = slab.astype(BF16)


def _proj(x, mod, gain, q_norm, k_norm, w, cast_ws=()):
    b, s, d = x.shape
    aw = HEADS * HEAD_DIM
    grid = (b, s // TILE)
    cast_specs, cast_shapes = _cast_specs(cast_ws, grid)
    row = lambda width: pl.BlockSpec((None, TILE, width), lambda bi, i: (bi, i, 0))
    return pl.pallas_call(
        functools.partial(_proj_kernel, len(cast_ws)),
        out_shape=([jax.ShapeDtypeStruct((b, s, aw), BF16)] * (3 * N_GROUPS) + [jax.ShapeDtypeStruct((b, s, d), BF16)] * 4
                   + cast_shapes),
        grid=grid,
        in_specs=[
            row(d),
            pl.BlockSpec((None, N_MOD, d), lambda bi, i: (bi, 0, 0)),
            _resident((1, d)),
            _resident((1, HEAD_DIM)),
            _resident((1, HEAD_DIM)),
            _resident(w.shape),
        ] + cast_specs,
        out_specs=[row(aw)] * (3 * N_GROUPS) + [row(d)] * 4 + cast_specs,
        scratch_shapes=[pltpu.VMEM((d // HEAD_DIM, TILE, HEAD_DIM), F32), pltpu.VMEM((d // HEAD_DIM, TILE, HEAD_DIM), F32),
                        pltpu.VMEM((2, TILE, d), BF16)],
        compiler_params=_params(2),
        name="proj",
    )(x, mod, gain, q_norm, k_norm, w, *cast_ws)


def _attn_block(q, kcat, vcat, bias):
    s = lax.dot_general(q, kcat, (((1,), (1,)), ((), ())), preferred_element_type=F32) + bias
    m = jnp.max(s, axis=-1, keepdims=True)
    p = jnp.exp2(s - m)
    od = _dot(p.astype(BF16), jnp.concatenate([vcat, jnp.ones_like(vcat)], axis=1))
    den = od[:, HEAD_DIM:]
    return od[:, :HEAD_DIM] / den, m + jnp.log2(den)


def _band_rows(dil, r, jb):
    per_tile = TILE // dil
    if dil == 1:
        return [(jb * BAND, BAND)]
    pos = r if dil == 4 else (r % 4) * 4 + r // 4
    tiles = BAND // per_tile
    return [((jb * tiles + t) * TILE + pos * per_tile, per_tile) for t in range(tiles)]


def _gather(ref, ranges):
    parts = [ref[a:a + n, :] for a, n in ranges]
    return parts[0] if len(parts) == 1 else jnp.concatenate(parts, axis=0)


def _attn_kernel(*refs):
    ins = refs[:3 * N_GROUPS]
    out_ref = refs[3 * N_GROUPS]
    scr = refs[3 * N_GROUPS + 1:]
    kprev, vprev = scr[0:N_GROUPS], scr[N_GROUPS:2 * N_GROUPS]
    o_nat, l_nat = scr[2 * N_GROUPS:3 * N_GROUPS], scr[3 * N_GROUPS:4 * N_GROUPS]
    o_tmp, l_tmp = scr[4 * N_GROUPS:4 * N_GROUPS + 2]
    c = pl.program_id(2)

    @pl.when(c == 0)
    def _():
        for ref in kprev + vprev:
            ref[...] = jnp.zeros(ref.shape, ref.dtype)

    qi = lax.broadcasted_iota(jnp.int32, (BAND, 2 * BAND), 0)
    kj = lax.broadcasted_iota(jnp.int32, (BAND, 2 * BAND), 1)
    in_prev = (kj < BAND) & (kj >= qi)
    in_cur = (kj >= BAND) & (kj - BAND <= qi)
    neg = jnp.float32(-jnp.inf)
    bias_in = jnp.where(in_prev | in_cur, 0.0, neg)
    bias_first = jnp.where(in_cur, 0.0, neg)
    bias_edge = jnp.where(c > 0, bias_in, bias_first)

    for g, dil in enumerate(DILATIONS):
        q_ref, k_ref, v_ref = ins[3 * g:3 * g + 3]
        blocks = CHUNK // dil // BAND
        carry = [rg for r in range(dil) for rg in _band_rows(dil, r, blocks - 1)]
        for r in range(dil):
            for jb in range(blocks):
                cur = _band_rows(dil, r, jb)
                q = _gather(q_ref, cur)
                if jb == 0:
                    base = r * BAND
                    k_prev = kprev[g][base:base + BAND, :]
                    v_prev = vprev[g][base:base + BAND, :]
                    bias = bias_edge
                else:
                    prev = _band_rows(dil, r, jb - 1)
                    k_prev, v_prev = _gather(k_ref, prev), _gather(v_ref, prev)
                    bias = bias_in
                kcat = jnp.concatenate([k_prev, _gather(k_ref, cur)], axis=0)
                vcat = jnp.concatenate([v_prev, _gather(v_ref, cur)], axis=0)
                o, lse = _attn_block(q, kcat, vcat, bias)
                if dil == 16:
                    rows = pl.ds((r % 4) * (CHUNK // 4) + r // 4, BAND, stride=4)
                    o_tmp[rows, :] = o
                    l_tmp[rows, :] = lse
                else:
                    rows = pl.ds(jb * BAND, BAND) if dil == 1 else pl.ds(jb * BAND * dil + r, BAND, stride=dil)
                    o_nat[g][rows, :] = o
                    l_nat[g][rows, :] = lse
        if dil == 16:
            seg = CHUNK // 4
            for r4 in range(4):
                o_nat[g][pl.ds(r4, seg, stride=4), :] = o_tmp[r4 * seg:(r4 + 1) * seg, :]
                l_nat[g][pl.ds(r4, seg, stride=4), :] = l_tmp[r4 * seg:(r4 + 1) * seg, :]
        off = 0
        for a, n in carry:
            kprev[g][off:off + n, :] = k_ref[a:a + n, :]
            vprev[g][off:off + n, :] = v_ref[a:a + n, :]
            off += n

    step = 256
    for a in range(0, CHUNK, step):
        ls = [l_nat[g][a:a + step, :] for g in range(N_GROUPS)]
        m = jnp.maximum(jnp.maximum(ls[0], ls[1]), ls[2])
        es = [jnp.exp2(l - m) for l in ls]
        num = es[0] * o_nat[0][a:a + step, :] + es[1] * o_nat[1][a:a + step, :] + es[2] * o_nat[2][a:a + step, :]
        out_ref[a:a + step, :] = (num / (es[0] + es[1] + es[2])).astype(BF16)


def _attn(qkv):
    b, s, aw = qkv[0].shape
    blk = pl.BlockSpec((None, CHUNK, HEAD_DIM), lambda bi, h, c: (bi, c, h))
    carry_rows = [BAND * dil for dil in DILATIONS]
    return pl.pallas_call(
        _attn_kernel,
        out_shape=jax.ShapeDtypeStruct((b, s, aw), BF16),
        grid=(b, HEADS, s // CHUNK),
        in_specs=[blk] * len(qkv),
        out_specs=blk,
        scratch_shapes=(
            [pltpu.VMEM((n, HEAD_DIM), BF16) for n in carry_rows] * 2
            + [pltpu.VMEM((CHUNK, HEAD_DIM), F32)] * (2 * N_GROUPS + 2)),
        compiler_params=_params(3),
        name="attn",
    )(*qkv)


def _mixout_kernel(x_ref, mod_ref, o_ref, xc_ref, halo_ref, bg_ref, sga_ref, sgc_ref,
                   cw_ref, wa_ref, wc_ref, wo_ref, out_ref):
    i = pl.program_id(1)
    pad = halo_ref.shape[0]
    for r0 in range(0, x_ref.shape[0], MIX_SUB_ROWS):
        rows = slice(r0, r0 + MIX_SUB_ROWS)
        y_attn = _dot(o_ref[rows, :], wa_ref[...])
        xc = xc_ref[rows, :].astype(F32)
        if r0 == 0:
            before = jnp.where(i > 0, halo_ref[...].astype(F32), 0.0)
        else:
            before = xc_ref[r0 - pad:r0, :].astype(F32)
        ext = jnp.concatenate([before, xc], axis=0)
        conv = xc * cw_ref[CONV_K - 1:CONV_K, :]
        for j in range(CONV_K - 1):
            back = CONV_K - 1 - j
            conv = conv + ext[pad - back:pad - back + MIX_SUB_ROWS, :] * cw_ref[j:j + 1, :]
        y_conv = _dot((bg_ref[rows, :].astype(F32) * conv).astype(BF16), wc_ref[...])
        merged = (sga_ref[rows, :].astype(F32) * y_attn + sgc_ref[rows, :].astype(F32) * y_conv).astype(BF16)
        out_ref[rows, :] = x_ref[rows, :] + mod_ref[5:6, :] * _dot(merged, wo_ref[...])


def _mixout(x, mod, o, xc, bg, sga, sgc, conv_w, w_attn, w_conv, w_out, tm=1024):
    b, s, d = x.shape
    halo_rows = BF16_SUBLANES
    row = lambda width: pl.BlockSpec((None, tm, width), lambda bi, i: (bi, i, 0))
    halo = pl.BlockSpec((None, halo_rows, d),
                        lambda bi, i: (bi, jnp.maximum(i * (tm // halo_rows) - 1, 0), 0))
    return pl.pallas_call(
        _mixout_kernel,
        out_shape=jax.ShapeDtypeStruct(x.shape, F32),
        grid=(b, s // tm),
        in_specs=[row(d), pl.BlockSpec((None, N_MOD, d), lambda bi, i: (bi, 0, 0)), row(o.shape[-1]),
                  row(d), halo, row(d), row(d), row(d),
                  _resident(conv_w.shape), _resident(w_attn.shape), _resident(w_conv.shape), _resident(w_out.shape)],
        out_specs=row(d),
        compiler_params=_params(2),
        name="mixout",
    )(x, mod, o, xc, xc, bg, sga, sgc, conv_w, w_attn, w_conv, w_out)


def kernel(x, c, w_ada, b_ada, norm_ffn1, ffn1_w_gate, ffn1_w_up, ffn1_w_down, norm_mix, w_in, q_norm, k_norm,
           conv_w, w_attn_branch, w_conv_branch, w_out, norm_ffn2, ffn2_w_gate, ffn2_w_up, ffn2_w_down):
    b, s, d = x.shape
    depth = w_ada.shape[0]
    c_pad = jnp.zeros((8, d), F32).at[:b].set(c)
    for l in range(depth):
        mod, w_gate1, w_up1, w_down1 = _adaln(c_pad, w_ada[l], b_ada[l][None, :],
                                              cast_ws=(ffn1_w_gate[l], ffn1_w_up[l], ffn1_w_down[l]))
        mod = mod[:b].reshape(b, N_MOD, d)
        x, w_in_b = _ffn(x, mod, norm_ffn1[l][None, :], w_gate1, w_up1, w_down1, mod_row=0, cast_ws=(w_in[l],))
        res = _proj(x, mod, norm_mix[l][None, :], q_norm[l][None, :], k_norm[l][None, :], w_in_b,
                    cast_ws=(w_attn_branch[l], w_conv_branch[l], w_out[l], ffn2_w_gate[l], ffn2_w_up[l], ffn2_w_down[l]))
        n_qkv = 3 * N_GROUPS
        o = _attn(res[:n_qkv])
        w_attn_b, w_conv_b, w_out_b, w_gate_b, w_up_b, w_down_b = res[n_qkv + 4:]
        x = _mixout(x, mod, o, *res[n_qkv:n_qkv + 4], conv_w[l], w_attn_b, w_conv_b, w_out_b)
        x, = _ffn(x, mod, norm_ffn2[l][None, :], w_gate_b, w_up_b, w_down_b, mod_row=6)
    return x
```

```python
import functools

import jax
import jax.numpy as jnp
from jax import lax
from jax.experimental import pallas as pl
from jax.experimental.pallas import tpu as pltpu

EPS = 1e-6
N_MOD = 9
N_GROUPS = 3
HEADS = 4
HEAD_DIM = 128
DILATIONS = (1, 4, 16)
BAND = 128
CHUNK = BAND * max(DILATIONS)
TILE = 512
FFN_SUB_ROWS = 256
MIX_SUB_ROWS = 256
CONV_K = 3
LOG2_E = 1.4426950408889634

V7X_VMEM_BYTES = 64 * 1024 * 1024
VMEM_LIMIT = V7X_VMEM_BYTES - 8 * 1024 * 1024

BF16 = jnp.bfloat16
BF16_SUBLANES = 16
F32_SUBLANES = 8
F32 = jnp.float32


def _resident(shape):
    return pl.BlockSpec(shape, lambda *_: (0,) * len(shape), pipeline_mode=pl.Buffered(1))


def _params(n_axes):
    return pltpu.CompilerParams(dimension_semantics=("arbitrary",) * n_axes, vmem_limit_bytes=VMEM_LIMIT)


def _rms_mod(x, gain, shift, scale):
    ms = jnp.mean(x * x, axis=-1, keepdims=True)
    return x * lax.rsqrt(ms + EPS) * (gain * (1.0 + scale)) + shift


def _dot(a, b):
    return jnp.dot(a, b, preferred_element_type=F32)


def _cast_specs(ws, grid):
    n_steps = 1
    for g in grid:
        n_steps *= g
    specs, shapes = [], []
    for w in ws:
        rows, cols = w.shape
        n_bands = n_steps
        while rows % n_bands or (rows // n_bands) % BF16_SUBLANES:
            n_bands //= 2
        per = n_steps // n_bands

        def index(*ids, per=per):
            flat = ids[0]
            for i, g in zip(ids[1:], grid[1:]):
                flat = flat * g + i
            return (flat // per, 0)

        specs.append(pl.BlockSpec((rows // n_bands, cols), index))
        shapes.append(jax.ShapeDtypeStruct(w.shape, BF16))
    return specs, shapes


def _cast_bands(src_refs, dst_refs):
    for src, dst in zip(src_refs, dst_refs):
        dst[...] = src[...].astype(BF16)


def _adaln_kernel(n_cast, c_ref, w_ref, b_ref, *rest):
    o_ref = rest[n_cast]
    _cast_bands(rest[:n_cast], rest[n_cast + 1:])
    c = c_ref[...]
    c_act = c * jax.nn.sigmoid(c)
    o_ref[...] = _dot(c_act, w_ref[...]) + b_ref[...]


def _adaln(c_pad, w_ada, b_ada, cast_ws=(), tn=1152):
    rows, d = c_pad.shape
    n = w_ada.shape[1]
    grid = (n // tn,)
    cast_specs, cast_shapes = _cast_specs(cast_ws, grid)
    return pl.pallas_call(
        functools.partial(_adaln_kernel, len(cast_ws)),
        out_shape=[jax.ShapeDtypeStruct((rows, n), F32)] + cast_shapes,
        grid=grid,
        in_specs=[
            pl.BlockSpec((rows, d), lambda j: (0, 0)),
            pl.BlockSpec((d, tn), lambda j: (0, j)),
            pl.BlockSpec((1, tn), lambda j: (0, j)),
        ] + cast_specs,
        out_specs=[pl.BlockSpec((rows, tn), lambda j: (0, j))] + cast_specs,
        compiler_params=_params(1),
        name="adaln",
    )(c_pad, w_ada, b_ada, *cast_ws)


def _ffn_kernel(mod_row, n_cast, x_ref, mod_ref, g_ref, wg_ref, wu_ref, wd_ref, *rest):
    o_ref = rest[n_cast]
    _cast_bands(rest[:n_cast], rest[n_cast + 1:])
    shift = mod_ref[mod_row:mod_row + 1, :]
    scale = mod_ref[mod_row + 1:mod_row + 2, :]
    gate = mod_ref[mod_row + 2:mod_row + 3, :]
    for r0 in range(0, x_ref.shape[0], FFN_SUB_ROWS):
        x = x_ref[r0:r0 + FFN_SUB_ROWS, :]
        h = _rms_mod(x, g_ref[...], shift, scale).astype(BF16)
        a = _dot(h, wg_ref[...])
        u = _dot(h, wu_ref[...])
        act = (a * jax.nn.sigmoid(a) * u).astype(BF16)
        y = _dot(act, wd_ref[...])
        o_ref[r0:r0 + FFN_SUB_ROWS, :] = x + 0.5 * gate * y


def _ffn(x, mod, gain, w_gate, w_up, w_down, mod_row, cast_ws=(), tm=1024):
    b, s, d = x.shape
    f = w_gate.shape[1]
    grid = (b, s // tm)
    cast_specs, cast_shapes = _cast_specs(cast_ws, grid)
    row = pl.BlockSpec((None, tm, d), lambda bi, i: (bi, i, 0))
    return pl.pallas_call(
        functools.partial(_ffn_kernel, mod_row, len(cast_ws)),
        out_shape=[jax.ShapeDtypeStruct(x.shape, F32)] + cast_shapes,
        grid=grid,
        in_specs=[
            row,
            pl.BlockSpec((None, N_MOD, d), lambda bi, i: (bi, 0, 0)),
            _resident((1, d)),
            _resident((d, f)),
            _resident((d, f)),
            _resident((f, d)),
        ] + cast_specs,
        out_specs=[row] + cast_specs,
        compiler_params=_params(2),
        name="ffn",
    )(x, mod, gain, w_gate, w_up, w_down, *cast_ws)


def _regroup_rows(src, dst_f32, dst_bf16, seg):
    q = seg // 4
    for base in range(0, TILE, seg):
        for r in range(4):
            lo = base + r * q
            for j in range(src.shape[0]):
                piece = src[j, pl.ds(base + r, q, stride=4), :]
                if dst_f32 is not None:
                    dst_f32[j, lo:lo + q, :] = piece
                dst_bf16[lo:lo + q, j * HEAD_DIM:(j + 1) * HEAD_DIM] = piece.astype(BF16)


def _proj_kernel(n_cast, x_ref, mod_ref, g_ref, qn_ref, kn_ref, cw_ref, w_ref, *rest):
    cast_in, rest = rest[:n_cast], rest[n_cast:]
    qkv_refs = rest[:3 * N_GROUPS]
    yc_ref, sga_ref, sgc_ref = rest[3 * N_GROUPS:3 * N_GROUPS + 3]
    cast_out = rest[3 * N_GROUPS + 3:3 * N_GROUPS + 3 + n_cast]
    scr_a, scr_b, h_scr, xc_scr = rest[3 * N_GROUPS + 3 + n_cast:]
    _cast_bands(cast_in, cast_out)

    @pl.when(pl.program_id(1) == 0)
    def _():
        xc_scr[:F32_SUBLANES, :] = jnp.zeros((F32_SUBLANES, xc_scr.shape[1]), xc_scr.dtype)

    d = g_ref.shape[-1]
    aw = HEADS * HEAD_DIM
    hf = _rms_mod(x_ref[...], g_ref[...], mod_ref[3:4, :], mod_ref[4:5, :])
    h = hf.astype(BF16)
    for j in range(d // HEAD_DIM):
        scr_a[j] = hf[:, j * HEAD_DIM:(j + 1) * HEAD_DIM]
    _regroup_rows(scr_a, scr_b, h_scr.at[0], TILE)
    _regroup_rows(scr_b, None, h_scr.at[1], TILE // 4)
    o = 3 * N_GROUPS * aw
    sga_ref[...] = jax.nn.sigmoid(_dot(h, w_ref[:, o + 3 * d:o + 4 * d])).astype(BF16)
    sgc_ref[...] = jax.nn.sigmoid(_dot(h, w_ref[:, o + 4 * d:o + 5 * d])).astype(BF16)
    pad = xc_scr.shape[0] - TILE
    xc = _dot(h, w_ref[:, o + 2 * d:o + 3 * d]) * _dot(h, w_ref[:, o:o + d])
    xc_scr[pad:, :] = xc
    conv = xc * cw_ref[CONV_K - 1:CONV_K, :]
    for j in range(CONV_K - 1):
        back = CONV_K - 1 - j
        conv = conv + xc_scr[pad - back:pad - back + TILE, :] * cw_ref[j:j + 1, :]
    yc_ref[...] = (_dot(h, w_ref[:, o + d:o + 2 * d]) * conv).astype(BF16)
    xc_scr[:pad, :] = xc_scr[TILE:, :]
    gains = (qn_ref[...] * (HEAD_DIM ** -0.5 * LOG2_E), kn_ref[...], None)
    for j in range(3):
        for g in range(N_GROUPS):
            c0 = (j * N_GROUPS + g) * aw
            t = _dot(h if g == 0 else h_scr[g - 1], w_ref[:, c0:c0 + aw])
            out_ref = qkv_refs[3 * g + j]
            for hd in range(HEADS):
                col = hd * HEAD_DIM
                slab = t[:, col:col + HEAD_DIM]
                if gains[j] is not None:
                    ms = jnp.mean(slab * slab, axis=-1, keepdims=True)
                    slab = slab * lax.rsqrt(ms + EPS) * gains[j]
                out_ref[:, col:col + HEAD_DIM] = slab.astype(BF16)


def _proj(x, mod, gain, q_norm, k_norm, conv_w, w, cast_ws=()):
    b, s, d = x.shape
    aw = HEADS * HEAD_DIM
    grid = (b, s // TILE)
    cast_specs, cast_shapes = _cast_specs(cast_ws, grid)
    row = lambda width: pl.BlockSpec((None, TILE, width), lambda bi, i: (bi, i, 0))
    return pl.pallas_call(
        functools.partial(_proj_kernel, len(cast_ws)),
        out_shape=([jax.ShapeDtypeStruct((b, s, aw), BF16)] * (3 * N_GROUPS) + [jax.ShapeDtypeStruct((b, s, d), BF16)] * 3
                   + cast_shapes),
        grid=grid,
        in_specs=[
            row(d),
            pl.BlockSpec((None, N_MOD, d), lambda bi, i: (bi, 0, 0)),
            _resident((1, d)),
            _resident((1, HEAD_DIM)),
            _resident((1, HEAD_DIM)),
            _resident(conv_w.shape),
            _resident(w.shape),
        ] + cast_specs,
        out_specs=[row(aw)] * (3 * N_GROUPS) + [row(d)] * 3 + cast_specs,
        scratch_shapes=[pltpu.VMEM((d // HEAD_DIM, TILE, HEAD_DIM), F32), pltpu.VMEM((d // HEAD_DIM, TILE, HEAD_DIM), F32),
                        pltpu.VMEM((2, TILE, d), BF16), pltpu.VMEM((F32_SUBLANES + TILE, d), F32)],
        compiler_params=_params(2),
        name="proj",
    )(x, mod, gain, q_norm, k_norm, conv_w, w, *cast_ws)


def _attn_block(q, kcat, vcat, bias):
    s = lax.dot_general(q, kcat, (((1,), (1,)), ((), ())), preferred_element_type=F32) + bias
    m = jnp.max(s, axis=-1, keepdims=True)
    p = jnp.exp2(s - m)
    od = _dot(p.astype(BF16), jnp.concatenate([vcat, jnp.ones_like(vcat)], axis=1))
    den = od[:, HEAD_DIM:]
    return od[:, :HEAD_DIM] / den, m + jnp.log2(den)


def _band_rows(dil, r, jb):
    per_tile = TILE // dil
    if dil == 1:
        return [(jb * BAND, BAND)]
    pos = r if dil == 4 else (r % 4) * 4 + r // 4
    tiles = BAND // per_tile
    return [((jb * tiles + t) * TILE + pos * per_tile, per_tile) for t in range(tiles)]


def _gather(ref, ranges):
    parts = [ref[a:a + n, :] for a, n in ranges]
    return parts[0] if len(parts) == 1 else jnp.concatenate(parts, axis=0)


def _attn_kernel(*refs):
    ins = refs[:3 * N_GROUPS]
    out_ref = refs[3 * N_GROUPS]
    scr = refs[3 * N_GROUPS + 1:]
    kprev, vprev = scr[0:N_GROUPS], scr[N_GROUPS:2 * N_GROUPS]
    o_nat, l_nat = scr[2 * N_GROUPS:3 * N_GROUPS], scr[3 * N_GROUPS:4 * N_GROUPS]
    o_tmp, l_tmp = scr[4 * N_GROUPS:4 * N_GROUPS + 2]
    c = pl.program_id(2)

    @pl.when(c == 0)
    def _():
        for ref in kprev + vprev:
            ref[...] = jnp.zeros(ref.shape, ref.dtype)

    qi = lax.broadcasted_iota(jnp.int32, (BAND, 2 * BAND), 0)
    kj = lax.broadcasted_iota(jnp.int32, (BAND, 2 * BAND), 1)
    in_prev = (kj < BAND) & (kj >= qi)
    in_cur = (kj >= BAND) & (kj - BAND <= qi)
    neg = jnp.float32(-jnp.inf)
    bias_in = jnp.where(in_prev | in_cur, 0.0, neg)
    bias_first = jnp.where(in_cur, 0.0, neg)
    bias_edge = jnp.where(c > 0, bias_in, bias_first)

    for g, dil in enumerate(DILATIONS):
        q_ref, k_ref, v_ref = ins[3 * g:3 * g + 3]
        blocks = CHUNK // dil // BAND
        carry = [rg for r in range(dil) for rg in _band_rows(dil, r, blocks - 1)]
        for r in range(dil):
            for jb in range(blocks):
                cur = _band_rows(dil, r, jb)
                q = _gather(q_ref, cur)
                if jb == 0:
                    base = r * BAND
                    k_prev = kprev[g][base:base + BAND, :]
                    v_prev = vprev[g][base:base + BAND, :]
                    bias = bias_edge
                else:
                    prev = _band_rows(dil, r, jb - 1)
                    k_prev, v_prev = _gather(k_ref, prev), _gather(v_ref, prev)
                    bias = bias_in
                kcat = jnp.concatenate([k_prev, _gather(k_ref, cur)], axis=0)
                vcat = jnp.concatenate([v_prev, _gather(v_ref, cur)], axis=0)
                o, lse = _attn_block(q, kcat, vcat, bias)
                if dil == 16:
                    rows = pl.ds((r % 4) * (CHUNK // 4) + r // 4, BAND, stride=4)
                    o_tmp[rows, :] = o
                    l_tmp[rows, :] = lse
                else:
                    rows = pl.ds(jb * BAND, BAND) if dil == 1 else pl.ds(jb * BAND * dil + r, BAND, stride=dil)
                    o_nat[g][rows, :] = o
                    l_nat[g][rows, :] = lse
        if dil == 16:
            seg = CHUNK // 4
            for r4 in range(4):
                o_nat[g][pl.ds(r4, seg, stride=4), :] = o_tmp[r4 * seg:(r4 + 1) * seg, :]
                l_nat[g][pl.ds(r4, seg, stride=4), :] = l_tmp[r4 * seg:(r4 + 1) * seg, :]
        off = 0
        for a, n in carry:
            kprev[g][off:off + n, :] = k_ref[a:a + n, :]
            vprev[g][off:off + n, :] = v_ref[a:a + n, :]
            off += n

    step = 256
    for a in range(0, CHUNK, step):
        ls = [l_nat[g][a:a + step, :] for g in range(N_GROUPS)]
        m = jnp.maximum(jnp.maximum(ls[0], ls[1]), ls[2])
        es = [jnp.exp2(l - m) for l in ls]
        num = es[0] * o_nat[0][a:a + step, :] + es[1] * o_nat[1][a:a + step, :] + es[2] * o_nat[2][a:a + step, :]
        out_ref[a:a + step, :] = (num / (es[0] + es[1] + es[2])).astype(BF16)


def _attn(qkv):
    b, s, aw = qkv[0].shape
    blk = pl.BlockSpec((None, CHUNK, HEAD_DIM), lambda bi, h, c: (bi, c, h))
    carry_rows = [BAND * dil for dil in DILATIONS]
    return pl.pallas_call(
        _attn_kernel,
        out_shape=jax.ShapeDtypeStruct((b, s, aw), BF16),
        grid=(b, HEADS, s // CHUNK),
        in_specs=[blk] * len(qkv),
        out_specs=blk,
        scratch_shapes=(
            [pltpu.VMEM((n, HEAD_DIM), BF16) for n in carry_rows] * 2
            + [pltpu.VMEM((CHUNK, HEAD_DIM), F32)] * (2 * N_GROUPS + 2)),
        compiler_params=_params(3),
        name="attn",
    )(*qkv)


def _mixout_kernel(x_ref, mod_ref, o_ref, yc_ref, sga_ref, sgc_ref, wa_ref, wc_ref, wo_ref, out_ref):
    for r0 in range(0, x_ref.shape[0], MIX_SUB_ROWS):
        rows = slice(r0, r0 + MIX_SUB_ROWS)
        y_attn = _dot(o_ref[rows, :], wa_ref[...])
        y_conv = _dot(yc_ref[rows, :], wc_ref[...])
        merged = (sga_ref[rows, :].astype(F32) * y_attn + sgc_ref[rows, :].astype(F32) * y_conv).astype(BF16)
        out_ref[rows, :] = x_ref[rows, :] + mod_ref[5:6, :] * _dot(merged, wo_ref[...])


def _mixout(x, mod, o, yc, sga, sgc, w_attn, w_conv, w_out, tm=1024):
    b, s, d = x.shape
    row = lambda width: pl.BlockSpec((None, tm, width), lambda bi, i: (bi, i, 0))
    return pl.pallas_call(
        _mixout_kernel,
        out_shape=jax.ShapeDtypeStruct(x.shape, F32),
        grid=(b, s // tm),
        in_specs=[row(d), pl.BlockSpec((None, N_MOD, d), lambda bi, i: (bi, 0, 0)), row(o.shape[-1]),
                  row(d), row(d), row(d),
                  _resident(w_attn.shape), _resident(w_conv.shape), _resident(w_out.shape)],
        out_specs=row(d),
        compiler_params=_params(2),
        name="mixout",
    )(x, mod, o, yc, sga, sgc, w_attn, w_conv, w_out)


def kernel(x, c, w_ada, b_ada, norm_ffn1, ffn1_w_gate, ffn1_w_up, ffn1_w_down, norm_mix, w_in, q_norm, k_norm,
           conv_w, w_attn_branch, w_conv_branch, w_out, norm_ffn2, ffn2_w_gate, ffn2_w_up, ffn2_w_down):
    b, s, d = x.shape
    depth = w_ada.shape[0]
    c_pad = jnp.zeros((8, d), F32).at[:b].set(c)
    for l in range(depth):
        mod, w_gate1, w_up1, w_down1 = _adaln(c_pad, w_ada[l], b_ada[l][None, :],
                                              cast_ws=(ffn1_w_gate[l], ffn1_w_up[l], ffn1_w_down[l]))
        mod = mod[:b].reshape(b, N_MOD, d)
        x, w_in_b = _ffn(x, mod, norm_ffn1[l][None, :], w_gate1, w_up1, w_down1, mod_row=0, cast_ws=(w_in[l],))
        res = _proj(x, mod, norm_mix[l][None, :], q_norm[l][None, :], k_norm[l][None, :], conv_w[l], w_in_b,
                    cast_ws=(w_attn_branch[l], w_conv_branch[l], w_out[l], ffn2_w_gate[l], ffn2_w_up[l], ffn2_w_down[l]))
        n_qkv = 3 * N_GROUPS
        o = _attn(res[:n_qkv])
        w_attn_b, w_conv_b, w_out_b, w_gate_b, w_up_b, w_down_b = res[n_qkv + 3:]
        x = _mixout(x, mod, o, *res[n_qkv:n_qkv + 3], w_attn_b, w_conv_b, w_out_b)
        x, = _ffn(x, mod, norm_ffn2[l][None, :], w_gate_b, w_up_b, w_down_b, mod_row=6)
    return x
```

```python
import functools

import jax
import jax.numpy as jnp
from jax import lax
from jax.experimental import pallas as pl
from jax.experimental.pallas import tpu as pltpu

EPS = 1e-6
N_MOD = 9
N_GROUPS = 3
HEADS = 4
HEAD_DIM = 128
DILATIONS = (1, 4, 16)
BAND = 128
CHUNK = BAND * max(DILATIONS)
TILE = 512
FFN_SUB_ROWS = 256
MIX_SUB_ROWS = 256
CONV_K = 3
LOG2_E = 1.4426950408889634

V7X_VMEM_BYTES = 64 * 1024 * 1024
VMEM_LIMIT = V7X_VMEM_BYTES - 8 * 1024 * 1024

BF16 = jnp.bfloat16
BF16_SUBLANES = 16
F32_SUBLANES = 8
F32 = jnp.float32


def _resident(shape):
    return pl.BlockSpec(shape, lambda *_: (0,) * len(shape), pipeline_mode=pl.Buffered(1))


def _params(n_axes):
    return pltpu.CompilerParams(dimension_semantics=("arbitrary",) * n_axes, vmem_limit_bytes=VMEM_LIMIT)


def _rms_mod(x, gain, shift, scale):
    ms = jnp.mean(x * x, axis=-1, keepdims=True)
    return x * lax.rsqrt(ms + EPS) * (gain * (1.0 + scale)) + shift


def _dot(a, b):
    return jnp.dot(a, b, preferred_element_type=F32)


def _cast_specs(ws, grid):
    n_steps = 1
    for g in grid:
        n_steps *= g
    specs, shapes = [], []
    for w in ws:
        rows, cols = w.shape
        n_bands = n_steps
        while rows % n_bands or (rows // n_bands) % BF16_SUBLANES:
            n_bands //= 2
        per = n_steps // n_bands

        def index(*ids, per=per):
            flat = ids[0]
            for i, g in zip(ids[1:], grid[1:]):
                flat = flat * g + i
            return (flat // per, 0)

        specs.append(pl.BlockSpec((rows // n_bands, cols), index))
        shapes.append(jax.ShapeDtypeStruct(w.shape, BF16))
    return specs, shapes


def _cast_bands(src_refs, dst_refs):
    for src, dst in zip(src_refs, dst_refs):
        dst[...] = src[...].astype(BF16)


def _adaln_kernel(n_cast, c_ref, w_ref, b_ref, *rest):
    o_ref = rest[n_cast]
    _cast_bands(rest[:n_cast], rest[n_cast + 1:])
    c = c_ref[...]
    c_act = c * jax.nn.sigmoid(c)
    o_ref[...] = _dot(c_act, w_ref[...]) + b_ref[...]


def _adaln(c_pad, w_ada, b_ada, cast_ws=(), tn=1152):
    rows, d = c_pad.shape
    n = w_ada.shape[1]
    grid = (n // tn,)
    cast_specs, cast_shapes = _cast_specs(cast_ws, grid)
    return pl.pallas_call(
        functools.partial(_adaln_kernel, len(cast_ws)),
        out_shape=[jax.ShapeDtypeStruct((rows, n), F32)] + cast_shapes,
        grid=grid,
        in_specs=[
            pl.BlockSpec((rows, d), lambda j: (0, 0)),
            pl.BlockSpec((d, tn), lambda j: (0, j)),
            pl.BlockSpec((1, tn), lambda j: (0, j)),
        ] + cast_specs,
        out_specs=[pl.BlockSpec((rows, tn), lambda j: (0, j))] + cast_specs,
        compiler_params=_params(1),
        name="adaln",
    )(c_pad, w_ada, b_ada, *cast_ws)


def _ffn_kernel(mod_row, n_cast, x_ref, mod_ref, g_ref, wg_ref, wu_ref, wd_ref, *rest):
    o_ref = rest[n_cast]
    shift = mod_ref[mod_row:mod_row + 1, :]
    scale = mod_ref[mod_row + 1:mod_row + 2, :]
    gate = mod_ref[mod_row + 2:mod_row + 3, :]
    for r0 in range(0, x_ref.shape[0], FFN_SUB_ROWS):
        x = x_ref[r0:r0 + FFN_SUB_ROWS, :]
        h = _rms_mod(x, g_ref[...], shift, scale).astype(BF16)
        a = _dot(h, wg_ref[...])
        u = _dot(h, wu_ref[...])
        act = (a * jax.nn.sigmoid(a) * u).astype(BF16)
        y = _dot(act, wd_ref[...])
        o_ref[r0:r0 + FFN_SUB_ROWS, :] = x + 0.5 * gate * y
    _cast_bands(rest[:n_cast], rest[n_cast + 1:])


def _ffn(x, mod, gain, w_gate, w_up, w_down, mod_row, cast_ws=(), tm=1024):
    b, s, d = x.shape
    f = w_gate.shape[1]
    grid = (b, s // tm)
    cast_specs, cast_shapes = _cast_specs(cast_ws, grid)
    row = pl.BlockSpec((None, tm, d), lambda bi, i: (bi, i, 0))
    return pl.pallas_call(
        functools.partial(_ffn_kernel, mod_row, len(cast_ws)),
        out_shape=[jax.ShapeDtypeStruct(x.shape, F32)] + cast_shapes,
        grid=grid,
        in_specs=[
            row,
            pl.BlockSpec((None, N_MOD, d), lambda bi, i: (bi, 0, 0)),
            _resident((1, d)),
            _resident((d, f)),
            _resident((d, f)),
            _resident((f, d)),
        ] + cast_specs,
        out_specs=[row] + cast_specs,
        compiler_params=_params(2),
        name="ffn",
    )(x, mod, gain, w_gate, w_up, w_down, *cast_ws)


def _regroup_rows(src, dst_f32, dst_bf16, seg):
    q = seg // 4
    for base in range(0, TILE, seg):
        for r in range(4):
            lo = base + r * q
            for j in range(src.shape[0]):
                piece = src[j, pl.ds(base + r, q, stride=4), :]
                if dst_f32 is not None:
                    dst_f32[j, lo:lo + q, :] = piece
                dst_bf16[lo:lo + q, j * HEAD_DIM:(j + 1) * HEAD_DIM] = piece.astype(BF16)


def _proj_kernel(n_cast, x_ref, mod_ref, g_ref, qn_ref, kn_ref, cw_ref, w_ref, *rest):
    cast_in, rest = rest[:n_cast], rest[n_cast:]
    qkv_ref, mix_ref = rest[:2]
    cast_out = rest[2:2 + n_cast]
    scr_a, scr_b, h_scr, xc_scr = rest[2 + n_cast:]

    @pl.when(pl.program_id(1) == 0)
    def _():
        xc_scr[:F32_SUBLANES, :] = jnp.zeros((F32_SUBLANES, xc_scr.shape[1]), xc_scr.dtype)

    d = g_ref.shape[-1]
    aw = HEADS * HEAD_DIM
    hf = _rms_mod(x_ref[...], g_ref[...], mod_ref[3:4, :], mod_ref[4:5, :])
    h = hf.astype(BF16)
    for j in range(d // HEAD_DIM):
        scr_a[j] = hf[:, j * HEAD_DIM:(j + 1) * HEAD_DIM]
    _regroup_rows(scr_a, scr_b, h_scr.at[0], TILE)
    _regroup_rows(scr_b, None, h_scr.at[1], TILE // 4)
    o = 3 * N_GROUPS * aw
    mix_ref[:, d:2 * d] = jax.nn.sigmoid(_dot(h, w_ref[:, o + 3 * d:o + 4 * d])).astype(BF16)
    mix_ref[:, 2 * d:] = jax.nn.sigmoid(_dot(h, w_ref[:, o + 4 * d:o + 5 * d])).astype(BF16)
    pad = xc_scr.shape[0] - TILE
    xc = _dot(h, w_ref[:, o + 2 * d:o + 3 * d]) * _dot(h, w_ref[:, o:o + d])
    xc_scr[pad:, :] = xc
    conv = xc * cw_ref[CONV_K - 1:CONV_K, :]
    for j in range(CONV_K - 1):
        back = CONV_K - 1 - j
        conv = conv + xc_scr[pad - back:pad - back + TILE, :] * cw_ref[j:j + 1, :]
    mix_ref[:, :d] = (_dot(h, w_ref[:, o + d:o + 2 * d]) * conv).astype(BF16)
    xc_scr[:pad, :] = xc_scr[TILE:, :]
    gains = (qn_ref[...] * (HEAD_DIM ** -0.5 * LOG2_E), kn_ref[...], None)
    for j in range(3):
        for g in range(N_GROUPS):
            c0 = (j * N_GROUPS + g) * aw
            t = _dot(h if g == 0 else h_scr[g - 1], w_ref[:, c0:c0 + aw])
            for hd in range(HEADS):
                col = hd * HEAD_DIM
                slab = t[:, col:col + HEAD_DIM]
                if gains[j] is not None:
                    ms = jnp.mean(slab * slab, axis=-1, keepdims=True)
                    slab = slab * lax.rsqrt(ms + EPS) * gains[j]
                dst = (3 * g + j) * aw + col
                qkv_ref[:, dst:dst + HEAD_DIM] = slab.astype(BF16)
    _cast_bands(cast_in, cast_out)


def _proj(x, mod, gain, q_norm, k_norm, conv_w, w, cast_ws=()):
    b, s, d = x.shape
    aw = HEADS * HEAD_DIM
    grid = (b, s // TILE)
    cast_specs, cast_shapes = _cast_specs(cast_ws, grid)
    row = lambda width: pl.BlockSpec((None, TILE, width), lambda bi, i: (bi, i, 0))
    qkv_w, mix_w = 3 * N_GROUPS * aw, 3 * d
    return pl.pallas_call(
        functools.partial(_proj_kernel, len(cast_ws)),
        out_shape=[jax.ShapeDtypeStruct((b, s, qkv_w), BF16), jax.ShapeDtypeStruct((b, s, mix_w), BF16)] + cast_shapes,
        grid=grid,
        in_specs=[
            row(d),
            pl.BlockSpec((None, N_MOD, d), lambda bi, i: (bi, 0, 0)),
            _resident((1, d)),
            _resident((1, HEAD_DIM)),
            _resident((1, HEAD_DIM)),
            _resident(conv_w.shape),
            _resident(w.shape),
        ] + cast_specs,
        out_specs=[row(qkv_w), row(mix_w)] + cast_specs,
        scratch_shapes=[pltpu.VMEM((d // HEAD_DIM, TILE, HEAD_DIM), F32), pltpu.VMEM((d // HEAD_DIM, TILE, HEAD_DIM), F32),
                        pltpu.VMEM((2, TILE, d), BF16), pltpu.VMEM((F32_SUBLANES + TILE, d), F32)],
        compiler_params=_params(2),
        name="proj",
    )(x, mod, gain, q_norm, k_norm, conv_w, w, *cast_ws)


def _attn_block(q, kcat, vcat, bias):
    s = lax.dot_general(q, kcat, (((1,), (1,)), ((), ())), preferred_element_type=F32) + bias
    m = jnp.max(s, axis=-1, keepdims=True)
    p = jnp.exp2(s - m)
    od = _dot(p.astype(BF16), jnp.concatenate([vcat, jnp.ones_like(vcat)], axis=1))
    den = od[:, HEAD_DIM:]
    return od[:, :HEAD_DIM] / den, m + jnp.log2(den)


def _band_rows(dil, r, jb):
    per_tile = TILE // dil
    if dil == 1:
        return [(jb * BAND, BAND)]
    pos = r if dil == 4 else (r % 4) * 4 + r // 4
    tiles = BAND // per_tile
    return [((jb * tiles + t) * TILE + pos * per_tile, per_tile) for t in range(tiles)]


def _gather(ref, ranges):
    parts = [ref[a:a + n, :] for a, n in ranges]
    return parts[0] if len(parts) == 1 else jnp.concatenate(parts, axis=0)


def _attn_kernel(*refs):
    ins = refs[:3 * N_GROUPS]
    out_ref = refs[3 * N_GROUPS]
    scr = refs[3 * N_GROUPS + 1:]
    kprev, vprev = scr[0:N_GROUPS], scr[N_GROUPS:2 * N_GROUPS]
    o_nat, l_nat = scr[2 * N_GROUPS:3 * N_GROUPS], scr[3 * N_GROUPS:4 * N_GROUPS]
    o_tmp, l_tmp = scr[4 * N_GROUPS:4 * N_GROUPS + 2]
    c = pl.program_id(2)

    @pl.when(c == 0)
    def _():
        for ref in kprev + vprev:
            ref[...] = jnp.zeros(ref.shape, ref.dtype)

    qi = lax.broadcasted_iota(jnp.int32, (BAND, 2 * BAND), 0)
    kj = lax.broadcasted_iota(jnp.int32, (BAND, 2 * BAND), 1)
    in_prev = (kj < BAND) & (kj >= qi)
    in_cur = (kj >= BAND) & (kj - BAND <= qi)
    neg = jnp.float32(-jnp.inf)
    bias_in = jnp.where(in_prev | in_cur, 0.0, neg)
    bias_first = jnp.where(in_cur, 0.0, neg)
    bias_edge = jnp.where(c > 0, bias_in, bias_first)

    for g, dil in enumerate(DILATIONS):
        q_ref, k_ref, v_ref = ins[3 * g:3 * g + 3]
        blocks = CHUNK // dil // BAND
        carry = [rg for r in range(dil) for rg in _band_rows(dil, r, blocks - 1)]
        for r in range(dil):
            for jb in range(blocks):
                cur = _band_rows(dil, r, jb)
                q = _gather(q_ref, cur)
                if jb == 0:
                    base = r * BAND
                    k_prev = kprev[g][base:base + BAND, :]
                    v_prev = vprev[g][base:base + BAND, :]
                    bias = bias_edge
                else:
                    prev = _band_rows(dil, r, jb - 1)
                    k_prev, v_prev = _gather(k_ref, prev), _gather(v_ref, prev)
                    bias = bias_in
                kcat = jnp.concatenate([k_prev, _gather(k_ref, cur)], axis=0)
                vcat = jnp.concatenate([v_prev, _gather(v_ref, cur)], axis=0)
                o, lse = _attn_block(q, kcat, vcat, bias)
                if dil == 16:
                    rows = pl.ds((r % 4) * (CHUNK // 4) + r // 4, BAND, stride=4)
                    o_tmp[rows, :] = o
                    l_tmp[rows, :] = lse
                else:
                    rows = pl.ds(jb * BAND, BAND) if dil == 1 else pl.ds(jb * BAND * dil + r, BAND, stride=dil)
                    o_nat[g][rows, :] = o
                    l_nat[g][rows, :] = lse
        if dil == 16:
            seg = CHUNK // 4
            for r4 in range(4):
                o_nat[g][pl.ds(r4, seg, stride=4), :] = o_tmp[r4 * seg:(r4 + 1) * seg, :]
                l_nat[g][pl.ds(r4, seg, stride=4), :] = l_tmp[r4 * seg:(r4 + 1) * seg, :]
        off = 0
        for a, n in carry:
            kprev[g][off:off + n, :] = k_ref[a:a + n, :]
            vprev[g][off:off + n, :] = v_ref[a:a + n, :]
            off += n

    step = 256
    for a in range(0, CHUNK, step):
        ls = [l_nat[g][a:a + step, :] for g in range(N_GROUPS)]
        m = jnp.maximum(jnp.maximum(ls[0], ls[1]), ls[2])
        es = [jnp.exp2(l - m) for l in ls]
        num = es[0] * o_nat[0][a:a + step, :] + es[1] * o_nat[1][a:a + step, :] + es[2] * o_nat[2][a:a + step, :]
        out_ref[a:a + step, :] = (num / (es[0] + es[1] + es[2])).astype(BF16)


def _attn(qkv):
    b, s, _ = qkv.shape
    aw = HEADS * HEAD_DIM
    n_in = 3 * N_GROUPS
    head_blk = lambda idx: pl.BlockSpec((None, CHUNK, HEAD_DIM), lambda bi, h, c: (bi, c, idx * HEADS + h))
    carry_rows = [BAND * dil for dil in DILATIONS]
    return pl.pallas_call(
        _attn_kernel,
        out_shape=jax.ShapeDtypeStruct((b, s, aw), BF16),
        grid=(b, HEADS, s // CHUNK),
        in_specs=[head_blk(idx) for idx in range(n_in)],
        out_specs=head_blk(0),
        scratch_shapes=(
            [pltpu.VMEM((n, HEAD_DIM), BF16) for n in carry_rows] * 2
            + [pltpu.VMEM((CHUNK, HEAD_DIM), F32)] * (2 * N_GROUPS + 2)),
        compiler_params=_params(3),
        name="attn",
    )(*([qkv] * n_in))


def _mixout_kernel(x_ref, mod_ref, o_ref, yc_ref, sga_ref, sgc_ref, wa_ref, wc_ref, wo_ref, out_ref):
    for r0 in range(0, x_ref.shape[0], MIX_SUB_ROWS):
        rows = slice(r0, r0 + MIX_SUB_ROWS)
        y_attn = _dot(o_ref[rows, :], wa_ref[...])
        y_conv = _dot(yc_ref[rows, :], wc_ref[...])
        merged = (sga_ref[rows, :].astype(F32) * y_attn + sgc_ref[rows, :].astype(F32) * y_conv).astype(BF16)
        out_ref[rows, :] = x_ref[rows, :] + mod_ref[5:6, :] * _dot(merged, wo_ref[...])


def _mixout(x, mod, o, mix, w_attn, w_conv, w_out, tm=1024):
    b, s, d = x.shape
    row = lambda width, col=0: pl.BlockSpec((None, tm, width), lambda bi, i: (bi, i, col))
    return pl.pallas_call(
        _mixout_kernel,
        out_shape=jax.ShapeDtypeStruct(x.shape, F32),
        grid=(b, s // tm),
        in_specs=[row(d), pl.BlockSpec((None, N_MOD, d), lambda bi, i: (bi, 0, 0)), row(o.shape[-1]),
                  row(d, 0), row(d, 1), row(d, 2),
                  _resident(w_attn.shape), _resident(w_conv.shape), _resident(w_out.shape)],
        out_specs=row(d),
        compiler_params=_params(2),
        name="mixout",
    )(x, mod, o, mix, mix, mix, w_attn, w_conv, w_out)


def kernel(x, c, w_ada, b_ada, norm_ffn1, ffn1_w_gate, ffn1_w_up, ffn1_w_down, norm_mix, w_in, q_norm, k_norm,
           conv_w, w_attn_branch, w_conv_branch, w_out, norm_ffn2, ffn2_w_gate, ffn2_w_up, ffn2_w_down):
    b, s, d = x.shape
    depth = w_ada.shape[0]
    c_pad = jnp.zeros((8, d), F32).at[:b].set(c)
    for l in range(depth):
        mod, w_gate1, w_up1, w_down1 = _adaln(c_pad, w_ada[l], b_ada[l][None, :],
                                              cast_ws=(ffn1_w_gate[l], ffn1_w_up[l], ffn1_w_down[l]))
        mod = mod[:b].reshape(b, N_MOD, d)
        x, w_in_b = _ffn(x, mod, norm_ffn1[l][None, :], w_gate1, w_up1, w_down1, mod_row=0, cast_ws=(w_in[l],))
        qkv, mix, w_attn_b, w_conv_b, w_out_b, w_gate_b, w_up_b, w_down_b = _proj(
            x, mod, norm_mix[l][None, :], q_norm[l][None, :], k_norm[l][None, :], conv_w[l], w_in_b,
            cast_ws=(w_attn_branch[l], w_conv_branch[l], w_out[l], ffn2_w_gate[l], ffn2_w_up[l], ffn2_w_down[l]))
        o = _attn(qkv)
        x = _mixout(x, mod, o, mix, w_attn_b, w_conv_b, w_out_b)
        x, = _ffn(x, mod, norm_ffn2[l][None, :], w_gate_b, w_up_b, w_down_b, mod_row=6)
    return x
```

```python
import functools

import jax
import jax.numpy as jnp
from jax import lax
from jax.experimental import pallas as pl
from jax.experimental.pallas import tpu as pltpu

EPS = 1e-6
N_MOD = 9
N_GROUPS = 3
HEADS = 4
HEAD_DIM = 128
DILATIONS = (1, 4, 16)
BAND = 128
CHUNK = BAND * max(DILATIONS)
TILE = 512
FFN_SUB_ROWS = 256
MIX_SUB_ROWS = 256
CONV_K = 3
LOG2_E = 1.4426950408889634

V7X_VMEM_BYTES = 64 * 1024 * 1024
VMEM_LIMIT = V7X_VMEM_BYTES - 8 * 1024 * 1024

BF16 = jnp.bfloat16
BF16_SUBLANES = 16
F32_SUBLANES = 8
F32 = jnp.float32


def _resident(shape):
    return pl.BlockSpec(shape, lambda *_: (0,) * len(shape), pipeline_mode=pl.Buffered(1))


def _params(n_axes):
    return pltpu.CompilerParams(dimension_semantics=("arbitrary",) * n_axes, vmem_limit_bytes=VMEM_LIMIT)


def _rms_mod(x, gain, shift, scale):
    ms = jnp.mean(x * x, axis=-1, keepdims=True)
    return x * lax.rsqrt(ms + EPS) * (gain * (1.0 + scale)) + shift


def _dot(a, b):
    return jnp.dot(a, b, preferred_element_type=F32)


def _cast_specs(ws, grid):
    n_steps = 1
    for g in grid:
        n_steps *= g
    specs, shapes = [], []
    for w in ws:
        rows, cols = w.shape
        n_bands = 1
        while n_bands * 2 <= n_steps:
            n_bands *= 2
        while rows % n_bands or (rows // n_bands) % BF16_SUBLANES:
            n_bands //= 2
        per = n_steps // n_bands

        def index(*ids, per=per, last=n_bands - 1):
            flat = ids[0]
            for i, g in zip(ids[1:], grid[1:]):
                flat = flat * g + i
            return (jnp.minimum(flat // per, last), 0)

        specs.append(pl.BlockSpec((rows // n_bands, cols), index))
        shapes.append(jax.ShapeDtypeStruct(w.shape, BF16))
    return specs, shapes


def _cast_bands(src_refs, dst_refs):
    for src, dst in zip(src_refs, dst_refs):
        dst[...] = src[...].astype(BF16)


def _adaln_kernel(n_cast, c_ref, w_ref, b_ref, *rest):
    o_ref = rest[n_cast]
    c = c_ref[...]
    c_act = c * jax.nn.sigmoid(c)
    lhs = jnp.concatenate([c_act, jnp.zeros((F32_SUBLANES - c.shape[0], c.shape[1]), F32)], axis=0)
    o_ref[...] = _dot(lhs, w_ref[...]) + b_ref[...]
    _cast_bands(rest[:n_cast], rest[n_cast + 1:])


def _adaln(c, w_ada, b_ada, layer, cast_ws=()):
    b, d = c.shape
    grid = (N_MOD,)
    cast_specs, cast_shapes = _cast_specs(cast_ws, grid)
    return pl.pallas_call(
        functools.partial(_adaln_kernel, len(cast_ws)),
        out_shape=[jax.ShapeDtypeStruct((N_MOD, F32_SUBLANES, d), F32)] + cast_shapes,
        grid=grid,
        in_specs=[
            pl.BlockSpec((b, d), lambda j: (0, 0)),
            pl.BlockSpec((None, d, d), lambda j: (layer, 0, j)),
            pl.BlockSpec((1, d), lambda j: (layer, j)),
        ] + cast_specs,
        out_specs=[pl.BlockSpec((None, F32_SUBLANES, d), lambda j: (j, 0, 0))] + cast_specs,
        compiler_params=_params(1),
        name="adaln",
    )(c, w_ada, b_ada, *cast_ws)


def _mod_row(mod_ref, k):
    return mod_ref[k, pl.ds(pl.program_id(0), 1), :]


def _mod_spec(d):
    return _resident((N_MOD, F32_SUBLANES, d))


def _ffn_kernel(mod_row, n_cast, x_ref, mod_ref, g_ref, wg_ref, wu_ref, wd_ref, *rest):
    o_ref = rest[n_cast]
    shift = _mod_row(mod_ref, mod_row)
    scale = _mod_row(mod_ref, mod_row + 1)
    gate = _mod_row(mod_ref, mod_row + 2)
    for r0 in range(0, x_ref.shape[0], FFN_SUB_ROWS):
        x = x_ref[r0:r0 + FFN_SUB_ROWS, :]
        h = _rms_mod(x, g_ref[...], shift, scale).astype(BF16)
        a = _dot(h, wg_ref[...])
        u = _dot(h, wu_ref[...])
        act = (a * jax.nn.sigmoid(a) * u).astype(BF16)
        y = _dot(act, wd_ref[...])
        o_ref[r0:r0 + FFN_SUB_ROWS, :] = x + 0.5 * gate * y
    _cast_bands(rest[:n_cast], rest[n_cast + 1:])


def _ffn(x, mod, gain, w_gate, w_up, w_down, mod_row, cast_ws=(), tm=1024):
    b, s, d = x.shape
    f = w_gate.shape[1]
    grid = (b, s // tm)
    cast_specs, cast_shapes = _cast_specs(cast_ws, grid)
    row = pl.BlockSpec((None, tm, d), lambda bi, i: (bi, i, 0))
    return pl.pallas_call(
        functools.partial(_ffn_kernel, mod_row, len(cast_ws)),
        out_shape=[jax.ShapeDtypeStruct(x.shape, F32)] + cast_shapes,
        grid=grid,
        in_specs=[
            row,
            _mod_spec(d),
            _resident((1, d)),
            _resident((d, f)),
            _resident((d, f)),
            _resident((f, d)),
        ] + cast_specs,
        out_specs=[row] + cast_specs,
        compiler_params=_params(2),
        name="ffn",
    )(x, mod, gain, w_gate, w_up, w_down, *cast_ws)


def _regroup_rows(src, dst_f32, dst_bf16, seg):
    q = seg // 4
    for base in range(0, TILE, seg):
        for r in range(4):
            lo = base + r * q
            for j in range(src.shape[0]):
                piece = src[j, pl.ds(base + r, q, stride=4), :]
                if dst_f32 is not None:
                    dst_f32[j, lo:lo + q, :] = piece
                dst_bf16[lo:lo + q, j * HEAD_DIM:(j + 1) * HEAD_DIM] = piece.astype(BF16)


def _proj_kernel(n_cast, x_ref, mod_ref, g_ref, qn_ref, kn_ref, cw_ref, w_ref, *rest):
    cast_in, rest = rest[:n_cast], rest[n_cast:]
    qkv_ref, mix_ref = rest[:2]
    cast_out = rest[2:2 + n_cast]
    scr_a, scr_b, h_scr, xc_scr = rest[2 + n_cast:]

    @pl.when(pl.program_id(1) == 0)
    def _():
        xc_scr[:F32_SUBLANES, :] = jnp.zeros((F32_SUBLANES, xc_scr.shape[1]), xc_scr.dtype)

    d = g_ref.shape[-1]
    aw = HEADS * HEAD_DIM
    hf = _rms_mod(x_ref[...], g_ref[...], _mod_row(mod_ref, 3), _mod_row(mod_ref, 4))
    h = hf.astype(BF16)
    for j in range(d // HEAD_DIM):
        scr_a[j] = hf[:, j * HEAD_DIM:(j + 1) * HEAD_DIM]
    _regroup_rows(scr_a, scr_b, h_scr.at[0], TILE)
    _regroup_rows(scr_b, None, h_scr.at[1], TILE // 4)
    o = 3 * N_GROUPS * aw
    mix_ref[:, d:2 * d] = jax.nn.sigmoid(_dot(h, w_ref[:, o + 3 * d:o + 4 * d])).astype(BF16)
    mix_ref[:, 2 * d:] = jax.nn.sigmoid(_dot(h, w_ref[:, o + 4 * d:o + 5 * d])).astype(BF16)
    pad = xc_scr.shape[0] - TILE
    xc = _dot(h, w_ref[:, o + 2 * d:o + 3 * d]) * _dot(h, w_ref[:, o:o + d])
    xc_scr[pad:, :] = xc
    conv = xc * cw_ref[CONV_K - 1:CONV_K, :]
    for j in range(CONV_K - 1):
        back = CONV_K - 1 - j
        conv = conv + xc_scr[pad - back:pad - back + TILE, :] * cw_ref[j:j + 1, :]
    mix_ref[:, :d] = (_dot(h, w_ref[:, o + d:o + 2 * d]) * conv).astype(BF16)
    xc_scr[:pad, :] = xc_scr[TILE:, :]
    gains = (qn_ref[...] * (HEAD_DIM ** -0.5 * LOG2_E), kn_ref[...], None)
    for j in range(3):
        for g in range(N_GROUPS):
            c0 = (j * N_GROUPS + g) * aw
            t = _dot(h if g == 0 else h_scr[g - 1], w_ref[:, c0:c0 + aw])
            for hd in range(HEADS):
                col = hd * HEAD_DIM
                slab = t[:, col:col + HEAD_DIM]
                if gains[j] is not None:
                    ms = jnp.mean(slab * slab, axis=-1, keepdims=True)
                    slab = slab * lax.rsqrt(ms + EPS) * gains[j]
                dst = (3 * g + j) * aw + col
                qkv_ref[:, dst:dst + HEAD_DIM] = slab.astype(BF16)
    _cast_bands(cast_in, cast_out)


def _proj(x, mod, gain, q_norm, k_norm, conv_w, w, cast_ws=()):
    b, s, d = x.shape
    aw = HEADS * HEAD_DIM
    grid = (b, s // TILE)
    cast_specs, cast_shapes = _cast_specs(cast_ws, grid)
    row = lambda width: pl.BlockSpec((None, TILE, width), lambda bi, i: (bi, i, 0))
    qkv_w, mix_w = 3 * N_GROUPS * aw, 3 * d
    return pl.pallas_call(
        functools.partial(_proj_kernel, len(cast_ws)),
        out_shape=[jax.ShapeDtypeStruct((b, s, qkv_w), BF16), jax.ShapeDtypeStruct((b, s, mix_w), BF16)] + cast_shapes,
        grid=grid,
        in_specs=[
            row(d),
            _mod_spec(d),
            _resident((1, d)),
            _resident((1, HEAD_DIM)),
            _resident((1, HEAD_DIM)),
            _resident(conv_w.shape),
            _resident(w.shape),
        ] + cast_specs,
        out_specs=[row(qkv_w), row(mix_w)] + cast_specs,
        scratch_shapes=[pltpu.VMEM((d // HEAD_DIM, TILE, HEAD_DIM), F32), pltpu.VMEM((d // HEAD_DIM, TILE, HEAD_DIM), F32),
                        pltpu.VMEM((2, TILE, d), BF16), pltpu.VMEM((F32_SUBLANES + TILE, d), F32)],
        compiler_params=_params(2),
        name="proj",
    )(x, mod, gain, q_norm, k_norm, conv_w, w, *cast_ws)


def _attn_block(q, kcat, vcat, bias):
    s = lax.dot_general(q, kcat, (((1,), (1,)), ((), ())), preferred_element_type=F32) + bias
    m = jnp.max(s, axis=-1, keepdims=True)
    p = jnp.exp2(s - m)
    od = _dot(p.astype(BF16), jnp.concatenate([vcat, jnp.ones_like(vcat)], axis=1))
    den = od[:, HEAD_DIM:]
    return od[:, :HEAD_DIM] / den, m + jnp.log2(den)


def _band_rows(dil, r, jb):
    per_tile = TILE // dil
    if dil == 1:
        return [(jb * BAND, BAND)]
    pos = r if dil == 4 else (r % 4) * 4 + r // 4
    tiles = BAND // per_tile
    return [((jb * tiles + t) * TILE + pos * per_tile, per_tile) for t in range(tiles)]


def _gather(ref, ranges):
    parts = [ref[a:a + n, :] for a, n in ranges]
    return parts[0] if len(parts) == 1 else jnp.concatenate(parts, axis=0)


def _attn_kernel(*refs):
    ins = refs[:3 * N_GROUPS]
    out_ref = refs[3 * N_GROUPS]
    scr = refs[3 * N_GROUPS + 1:]
    kprev, vprev = scr[0:N_GROUPS], scr[N_GROUPS:2 * N_GROUPS]
    o_nat, l_nat = scr[2 * N_GROUPS:3 * N_GROUPS], scr[3 * N_GROUPS:4 * N_GROUPS]
    o_tmp, l_tmp = scr[4 * N_GROUPS:4 * N_GROUPS + 2]
    c = pl.program_id(2)

    @pl.when(c == 0)
    def _():
        for ref in kprev + vprev:
            ref[...] = jnp.zeros(ref.shape, ref.dtype)

    qi = lax.broadcasted_iota(jnp.int32, (BAND, 2 * BAND), 0)
    kj = lax.broadcasted_iota(jnp.int32, (BAND, 2 * BAND), 1)
    in_prev = (kj < BAND) & (kj >= qi)
    in_cur = (kj >= BAND) & (kj - BAND <= qi)
    neg = jnp.float32(-jnp.inf)
    bias_in = jnp.where(in_prev | in_cur, 0.0, neg)
    bias_first = jnp.where(in_cur, 0.0, neg)
    bias_edge = jnp.where(c > 0, bias_in, bias_first)

    for g, dil in enumerate(DILATIONS):
        q_ref, k_ref, v_ref = ins[3 * g:3 * g + 3]
        blocks = CHUNK // dil // BAND
        carry = [rg for r in range(dil) for rg in _band_rows(dil, r, blocks - 1)]
        for r in range(dil):
            for jb in range(blocks):
                cur = _band_rows(dil, r, jb)
                q = _gather(q_ref, cur)
                if jb == 0:
                    base = r * BAND
                    k_prev = kprev[g][base:base + BAND, :]
                    v_prev = vprev[g][base:base + BAND, :]
                    bias = bias_edge
                else:
                    prev = _band_rows(dil, r, jb - 1)
                    k_prev, v_prev = _gather(k_ref, prev), _gather(v_ref, prev)
                    bias = bias_in
                kcat = jnp.concatenate([k_prev, _gather(k_ref, cur)], axis=0)
                vcat = jnp.concatenate([v_prev, _gather(v_ref, cur)], axis=0)
                o, lse = _attn_block(q, kcat, vcat, bias)
                if dil == 16:
                    rows = pl.ds((r % 4) * (CHUNK // 4) + r // 4, BAND, stride=4)
                    o_tmp[rows, :] = o
                    l_tmp[rows, :] = lse
                else:
                    rows = pl.ds(jb * BAND, BAND) if dil == 1 else pl.ds(jb * BAND * dil + r, BAND, stride=dil)
                    o_nat[g][rows, :] = o
                    l_nat[g][rows, :] = lse
        if dil == 16:
            seg = CHUNK // 4
            for r4 in range(4):
                o_nat[g][pl.ds(r4, seg, stride=4), :] = o_tmp[r4 * seg:(r4 + 1) * seg, :]
                l_nat[g][pl.ds(r4, seg, stride=4), :] = l_tmp[r4 * seg:(r4 + 1) * seg, :]
        off = 0
        for a, n in carry:
            kprev[g][off:off + n, :] = k_ref[a:a + n, :]
            vprev[g][off:off + n, :] = v_ref[a:a + n, :]
            off += n

    step = 256
    for a in range(0, CHUNK, step):
        ls = [l_nat[g][a:a + step, :] for g in range(N_GROUPS)]
        m = jnp.maximum(jnp.maximum(ls[0], ls[1]), ls[2])
        es = [jnp.exp2(l - m) for l in ls]
        num = es[0] * o_nat[0][a:a + step, :] + es[1] * o_nat[1][a:a + step, :] + es[2] * o_nat[2][a:a + step, :]
        out_ref[a:a + step, :] = (num / (es[0] + es[1] + es[2])).astype(BF16)


def _attn(qkv):
    b, s, _ = qkv.shape
    aw = HEADS * HEAD_DIM
    n_in = 3 * N_GROUPS
    head_blk = lambda idx: pl.BlockSpec((None, CHUNK, HEAD_DIM), lambda bi, h, c: (bi, c, idx * HEADS + h))
    carry_rows = [BAND * dil for dil in DILATIONS]
    return pl.pallas_call(
        _attn_kernel,
        out_shape=jax.ShapeDtypeStruct((b, s, aw), BF16),
        grid=(b, HEADS, s // CHUNK),
        in_specs=[head_blk(idx) for idx in range(n_in)],
        out_specs=head_blk(0),
        scratch_shapes=(
            [pltpu.VMEM((n, HEAD_DIM), BF16) for n in carry_rows] * 2
            + [pltpu.VMEM((CHUNK, HEAD_DIM), F32)] * (2 * N_GROUPS + 2)),
        compiler_params=_params(3),
        name="attn",
    )(*([qkv] * n_in))


def _mixout_kernel(x_ref, mod_ref, o_ref, yc_ref, sga_ref, sgc_ref, wa_ref, wc_ref, wo_ref, out_ref):
    for r0 in range(0, x_ref.shape[0], MIX_SUB_ROWS):
        rows = slice(r0, r0 + MIX_SUB_ROWS)
        y_attn = _dot(o_ref[rows, :], wa_ref[...])
        y_conv = _dot(yc_ref[rows, :], wc_ref[...])
        merged = (sga_ref[rows, :].astype(F32) * y_attn + sgc_ref[rows, :].astype(F32) * y_conv).astype(BF16)
        out_ref[rows, :] = x_ref[rows, :] + _mod_row(mod_ref, 5) * _dot(merged, wo_ref[...])


def _mixout(x, mod, o, mix, w_attn, w_conv, w_out, tm=1024):
    b, s, d = x.shape
    row = lambda width, col=0: pl.BlockSpec((None, tm, width), lambda bi, i: (bi, i, col))
    return pl.pallas_call(
        _mixout_kernel,
        out_shape=jax.ShapeDtypeStruct(x.shape, F32),
        grid=(b, s // tm),
        in_specs=[row(d), _mod_spec(d), row(o.shape[-1]),
                  row(d, 0), row(d, 1), row(d, 2),
                  _resident(w_attn.shape), _resident(w_conv.shape), _resident(w_out.shape)],
        out_specs=row(d),
        compiler_params=_params(2),
        name="mixout",
    )(x, mod, o, mix, mix, mix, w_attn, w_conv, w_out)


def kernel(x, c, w_ada, b_ada, norm_ffn1, ffn1_w_gate, ffn1_w_up, ffn1_w_down, norm_mix, w_in, q_norm, k_norm,
           conv_w, w_attn_branch, w_conv_branch, w_out, norm_ffn2, ffn2_w_gate, ffn2_w_up, ffn2_w_down):
    b, s, d = x.shape
    depth = w_ada.shape[0]
    for l in range(depth):
        mod, w_gate1, w_up1, w_down1 = _adaln(c, w_ada, b_ada, l,
                                              cast_ws=(ffn1_w_gate[l], ffn1_w_up[l], ffn1_w_down[l]))
        x, w_in_b = _ffn(x, mod, norm_ffn1[l][None, :], w_gate1, w_up1, w_down1, mod_row=0, cast_ws=(w_in[l],))
        qkv, mix, w_attn_b, w_conv_b, w_out_b, w_gate_b, w_up_b, w_down_b = _proj(
            x, mod, norm_mix[l][None, :], q_norm[l][None, :], k_norm[l][None, :], conv_w[l], w_in_b,
            cast_ws=(w_attn_branch[l], w_conv_branch[l], w_out[l], ffn2_w_gate[l], ffn2_w_up[l], ffn2_w_down[l]))
        o = _attn(qkv)
        x = _mixout(x, mod, o, mix, w_attn_b, w_conv_b, w_out_b)
        x, = _ffn(x, mod, norm_ffn2[l][None, :], w_gate_b, w_up_b, w_down_b, mod_row=6)
    return x
```

```python
import functools

import jax
import jax.numpy as jnp
from jax import lax
from jax.experimental import pallas as pl
from jax.experimental.pallas import tpu as pltpu

EPS = 1e-6
N_MOD = 9
N_GROUPS = 3
HEADS = 4
HEAD_DIM = 128
DILATIONS = (1, 4, 16)
BAND = 128
CHUNK = BAND * max(DILATIONS)
TILE = 512
FFN_SUB_ROWS = 256
MIX_SUB_ROWS = 256
CONV_K = 3
LOG2_E = 1.4426950408889634

V7X_VMEM_BYTES = 64 * 1024 * 1024
VMEM_LIMIT = V7X_VMEM_BYTES - 8 * 1024 * 1024

BF16 = jnp.bfloat16
BF16_SUBLANES = 16
F32_SUBLANES = 8
F32 = jnp.float32


def _resident(shape):
    return pl.BlockSpec(shape, lambda *_: (0,) * len(shape), pipeline_mode=pl.Buffered(1))


def _params(n_axes):
    return pltpu.CompilerParams(dimension_semantics=("arbitrary",) * n_axes, vmem_limit_bytes=VMEM_LIMIT)


def _rms_mod(x, gain, shift, scale):
    ms = jnp.mean(x * x, axis=-1, keepdims=True)
    return x * lax.rsqrt(ms + EPS) * (gain * (1.0 + scale)) + shift


def _dot(a, b):
    return jnp.dot(a, b, preferred_element_type=F32)


def _cast_specs(ws, grid):
    n_steps = 1
    for g in grid:
        n_steps *= g
    specs, shapes = [], []
    for w in ws:
        rows, cols = w.shape
        n_bands = 1
        while n_bands * 2 <= n_steps:
            n_bands *= 2
        while rows % n_bands or (rows // n_bands) % BF16_SUBLANES:
            n_bands //= 2
        per = n_steps // n_bands

        def index(*ids, per=per, last=n_bands - 1):
            flat = ids[0]
            for i, g in zip(ids[1:], grid[1:]):
                flat = flat * g + i
            return (jnp.minimum(flat // per, last), 0)

        specs.append(pl.BlockSpec((rows // n_bands, cols), index))
        shapes.append(jax.ShapeDtypeStruct(w.shape, BF16))
    return specs, shapes


def _cast_bands(src_refs, dst_refs):
    for src, dst in zip(src_refs, dst_refs):
        dst[...] = src[...].astype(BF16)


def _adaln_kernel(n_cast, c_ref, w_ref, b_ref, *rest):
    o_ref = rest[n_cast]
    c = c_ref[...]
    c_act = c * jax.nn.sigmoid(c)
    lhs = jnp.concatenate([c_act, jnp.zeros((F32_SUBLANES - c.shape[0], c.shape[1]), F32)], axis=0)
    o_ref[...] = _dot(lhs, w_ref[...]) + b_ref[...]
    _cast_bands(rest[:n_cast], rest[n_cast + 1:])


def _adaln(c, w_ada, b_ada, layer, cast_ws=()):
    b, d = c.shape
    grid = (N_MOD,)
    cast_specs, cast_shapes = _cast_specs(cast_ws, grid)
    return pl.pallas_call(
        functools.partial(_adaln_kernel, len(cast_ws)),
        out_shape=[jax.ShapeDtypeStruct((N_MOD, F32_SUBLANES, d), F32)] + cast_shapes,
        grid=grid,
        in_specs=[
            pl.BlockSpec((b, d), lambda j: (0, 0)),
            pl.BlockSpec((None, d, d), lambda j: (layer, 0, j)),
            pl.BlockSpec((1, d), lambda j: (layer, j)),
        ] + cast_specs,
        out_specs=[pl.BlockSpec((None, F32_SUBLANES, d), lambda j: (j, 0, 0))] + cast_specs,
        compiler_params=_params(1),
        name="adaln",
    )(c, w_ada, b_ada, *cast_ws)


def _mod_row(mod_ref, k):
    return mod_ref[k, pl.ds(pl.program_id(0), 1), :]


def _mod_spec(d):
    return _resident((N_MOD, F32_SUBLANES, d))


def _ffn_kernel(mod_row, n_cast, x_ref, mod_ref, g_ref, wg_ref, wu_ref, wd_ref, *rest):
    o_ref = rest[n_cast]
    shift = _mod_row(mod_ref, mod_row)
    scale = _mod_row(mod_ref, mod_row + 1)
    gate = _mod_row(mod_ref, mod_row + 2)
    for r0 in range(0, x_ref.shape[0], FFN_SUB_ROWS):
        x = x_ref[r0:r0 + FFN_SUB_ROWS, :]
        h = _rms_mod(x, g_ref[...], shift, scale).astype(BF16)
        a = _dot(h, wg_ref[...])
        u = _dot(h, wu_ref[...])
        act = (a * jax.nn.sigmoid(a) * u).astype(BF16)
        y = _dot(act, wd_ref[...])
        o_ref[r0:r0 + FFN_SUB_ROWS, :] = x + 0.5 * gate * y
    _cast_bands(rest[:n_cast], rest[n_cast + 1:])


def _ffn(x, mod, gain, w_gate, w_up, w_down, mod_row, cast_ws=(), tm=1024):
    b, s, d = x.shape
    f = w_gate.shape[1]
    grid = (b, s // tm)
    cast_specs, cast_shapes = _cast_specs(cast_ws, grid)
    row = pl.BlockSpec((None, tm, d), lambda bi, i: (bi, i, 0))
    return pl.pallas_call(
        functools.partial(_ffn_kernel, mod_row, len(cast_ws)),
        out_shape=[jax.ShapeDtypeStruct(x.shape, F32)] + cast_shapes,
        grid=grid,
        in_specs=[
            row,
            _mod_spec(d),
            _resident((1, d)),
            _resident((d, f)),
            _resident((d, f)),
            _resident((f, d)),
        ] + cast_specs,
        out_specs=[row] + cast_specs,
        compiler_params=_params(2),
        name="ffn",
    )(x, mod, gain, w_gate, w_up, w_down, *cast_ws)


def _regroup_rows(src, dst_f32, dst_bf16, seg):
    q = seg // 4
    for base in range(0, TILE, seg):
        for r in range(4):
            lo = base + r * q
            for j in range(src.shape[0]):
                piece = src[j, pl.ds(base + r, q, stride=4), :]
                if dst_f32 is not None:
                    dst_f32[j, lo:lo + q, :] = piece
                dst_bf16[lo:lo + q, j * HEAD_DIM:(j + 1) * HEAD_DIM] = piece.astype(BF16)


def _proj_kernel(n_cast, x_ref, mod_ref, g_ref, qn_ref, kn_ref, cw_ref, w_ref, *rest):
    cast_in, rest = rest[:n_cast], rest[n_cast:]
    qkv_ref, mix_ref = rest[:2]
    cast_out = rest[2:2 + n_cast]
    scr_a, scr_b, h_scr, xc_scr = rest[2 + n_cast:]

    @pl.when(pl.program_id(1) == 0)
    def _():
        xc_scr[:F32_SUBLANES, :] = jnp.zeros((F32_SUBLANES, xc_scr.shape[1]), xc_scr.dtype)

    d = g_ref.shape[-1]
    aw = HEADS * HEAD_DIM
    hf = _rms_mod(x_ref[...], g_ref[...], _mod_row(mod_ref, 3), _mod_row(mod_ref, 4))
    h = hf.astype(BF16)
    for j in range(d // HEAD_DIM):
        scr_a[j] = hf[:, j * HEAD_DIM:(j + 1) * HEAD_DIM]
    _regroup_rows(scr_a, scr_b, h_scr.at[0], TILE)
    _regroup_rows(scr_b, None, h_scr.at[1], TILE // 4)
    o = 3 * N_GROUPS * aw
    mix_ref[:, d:2 * d] = jax.nn.sigmoid(_dot(h, w_ref[:, o + 3 * d:o + 4 * d])).astype(BF16)
    mix_ref[:, 2 * d:] = jax.nn.sigmoid(_dot(h, w_ref[:, o + 4 * d:o + 5 * d])).astype(BF16)
    pad = xc_scr.shape[0] - TILE
    xc = _dot(h, w_ref[:, o + 2 * d:o + 3 * d]) * _dot(h, w_ref[:, o:o + d])
    xc_scr[pad:, :] = xc
    conv = xc * cw_ref[CONV_K - 1:CONV_K, :]
    for j in range(CONV_K - 1):
        back = CONV_K - 1 - j
        conv = conv + xc_scr[pad - back:pad - back + TILE, :] * cw_ref[j:j + 1, :]
    mix_ref[:, :d] = (_dot(h, w_ref[:, o + d:o + 2 * d]) * conv).astype(BF16)
    xc_scr[:pad, :] = xc_scr[TILE:, :]
    gains = (qn_ref[...] * (HEAD_DIM ** -0.5 * LOG2_E), kn_ref[...], None)
    for j in range(3):
        for g in range(N_GROUPS):
            c0 = (j * N_GROUPS + g) * aw
            t = _dot(h if g == 0 else h_scr[g - 1], w_ref[:, c0:c0 + aw])
            for hd in range(HEADS):
                col = hd * HEAD_DIM
                slab = t[:, col:col + HEAD_DIM]
                if gains[j] is not None:
                    ms = jnp.mean(slab * slab, axis=-1, keepdims=True)
                    slab = slab * lax.rsqrt(ms + EPS) * gains[j]
                dst = (3 * g + j) * aw + col
                qkv_ref[:, dst:dst + HEAD_DIM] = slab.astype(BF16)
    _cast_bands(cast_in, cast_out)


def _proj(x, mod, gain, q_norm, k_norm, conv_w, w, cast_ws=()):
    b, s, d = x.shape
    aw = HEADS * HEAD_DIM
    grid = (b, s // TILE)
    cast_specs, cast_shapes = _cast_specs(cast_ws, grid)
    row = lambda width: pl.BlockSpec((None, TILE, width), lambda bi, i: (bi, i, 0))
    qkv_w, mix_w = 3 * N_GROUPS * aw, 3 * d
    return pl.pallas_call(
        functools.partial(_proj_kernel, len(cast_ws)),
        out_shape=[jax.ShapeDtypeStruct((b, s, qkv_w), BF16), jax.ShapeDtypeStruct((b, s, mix_w), BF16)] + cast_shapes,
        grid=grid,
        in_specs=[
            row(d),
            _mod_spec(d),
            _resident((1, d)),
            _resident((1, HEAD_DIM)),
            _resident((1, HEAD_DIM)),
            _resident(conv_w.shape),
            _resident(w.shape),
        ] + cast_specs,
        out_specs=[row(qkv_w), row(mix_w)] + cast_specs,
        scratch_shapes=[pltpu.VMEM((d // HEAD_DIM, TILE, HEAD_DIM), F32), pltpu.VMEM((d // HEAD_DIM, TILE, HEAD_DIM), F32),
                        pltpu.VMEM((2, TILE, d), BF16), pltpu.VMEM((F32_SUBLANES + TILE, d), F32)],
        compiler_params=_params(2),
        name="proj",
    )(x, mod, gain, q_norm, k_norm, conv_w, w, *cast_ws)


def _attn_block(q, kcat, vcat, bias):
    s = lax.dot_general(q, kcat, (((1,), (1,)), ((), ())), preferred_element_type=F32) + bias
    m = jnp.max(s, axis=-1, keepdims=True)
    p = jnp.exp2(s - m)
    od = _dot(p.astype(BF16), jnp.concatenate([vcat, jnp.ones_like(vcat)], axis=1))
    den = od[:, HEAD_DIM:]
    return od[:, :HEAD_DIM] / den, m + jnp.log2(den)


def _band_rows(dil, r, jb):
    per_tile = TILE // dil
    if dil == 1:
        return [(jb * BAND, BAND)]
    pos = r if dil == 4 else (r % 4) * 4 + r // 4
    tiles = BAND // per_tile
    return [((jb * tiles + t) * TILE + pos * per_tile, per_tile) for t in range(tiles)]


def _gather(ref, ranges):
    parts = [ref[a:a + n, :] for a, n in ranges]
    return parts[0] if len(parts) == 1 else jnp.concatenate(parts, axis=0)


def _attn_kernel(*refs):
    ins = refs[:3 * N_GROUPS]
    out_ref = refs[3 * N_GROUPS]
    scr = refs[3 * N_GROUPS + 1:]
    kprev, vprev = scr[0:N_GROUPS], scr[N_GROUPS:2 * N_GROUPS]
    o_nat = (None,) + tuple(scr[2 * N_GROUPS:3 * N_GROUPS - 1])
    l_nat = (None,) + tuple(scr[3 * N_GROUPS - 1:4 * N_GROUPS - 2])
    o_tmp, l_tmp = scr[4 * N_GROUPS - 2:4 * N_GROUPS]
    c = pl.program_id(2)

    @pl.when(c == 0)
    def _():
        for ref in kprev + vprev:
            ref[...] = jnp.zeros(ref.shape, ref.dtype)

    qi = lax.broadcasted_iota(jnp.int32, (BAND, 2 * BAND), 0)
    kj = lax.broadcasted_iota(jnp.int32, (BAND, 2 * BAND), 1)
    in_prev = (kj < BAND) & (kj >= qi)
    in_cur = (kj >= BAND) & (kj - BAND <= qi)
    neg = jnp.float32(-jnp.inf)
    bias_in = jnp.where(in_prev | in_cur, 0.0, neg)
    bias_first = jnp.where(in_cur, 0.0, neg)
    bias_edge = jnp.where(c > 0, bias_in, bias_first)

    for g in tuple(range(1, N_GROUPS)) + (0,):
        dil = DILATIONS[g]
        q_ref, k_ref, v_ref = ins[3 * g:3 * g + 3]
        blocks = CHUNK // dil // BAND
        carry = [rg for r in range(dil) for rg in _band_rows(dil, r, blocks - 1)]
        for r in range(dil):
            for jb in range(blocks):
                cur = _band_rows(dil, r, jb)
                q = _gather(q_ref, cur)
                if jb == 0:
                    base = r * BAND
                    k_prev = kprev[g][base:base + BAND, :]
                    v_prev = vprev[g][base:base + BAND, :]
                    bias = bias_edge
                else:
                    prev = _band_rows(dil, r, jb - 1)
                    k_prev, v_prev = _gather(k_ref, prev), _gather(v_ref, prev)
                    bias = bias_in
                kcat = jnp.concatenate([k_prev, _gather(k_ref, cur)], axis=0)
                vcat = jnp.concatenate([v_prev, _gather(v_ref, cur)], axis=0)
                o, lse = _attn_block(q, kcat, vcat, bias)
                if dil == 16:
                    rows = pl.ds((r % 4) * (CHUNK // 4) + r // 4, BAND, stride=4)
                    o_tmp[rows, :] = o
                    l_tmp[rows, :] = lse
                elif dil > 1:
                    rows = pl.ds(jb * BAND * dil + r, BAND, stride=dil)
                    o_nat[g][rows, :] = o
                    l_nat[g][rows, :] = lse
                else:
                    a = jb * BAND
                    ls = [lse] + [l_nat[gg][a:a + BAND, :] for gg in range(1, N_GROUPS)]
                    os_ = [o] + [o_nat[gg][a:a + BAND, :] for gg in range(1, N_GROUPS)]
                    m = functools.reduce(jnp.maximum, ls)
                    es = [jnp.exp2(l - m) for l in ls]
                    num = sum(e * og for e, og in zip(es, os_))
                    out_ref[a:a + BAND, :] = (num / sum(es)).astype(BF16)
        if dil == 16:
            seg = CHUNK // 4
            for r4 in range(4):
                o_nat[g][pl.ds(r4, seg, stride=4), :] = o_tmp[r4 * seg:(r4 + 1) * seg, :]
                l_nat[g][pl.ds(r4, seg, stride=4), :] = l_tmp[r4 * seg:(r4 + 1) * seg, :]
        off = 0
        for a, n in carry:
            kprev[g][off:off + n, :] = k_ref[a:a + n, :]
            vprev[g][off:off + n, :] = v_ref[a:a + n, :]
            off += n


def _attn(qkv):
    b, s, _ = qkv.shape
    aw = HEADS * HEAD_DIM
    n_in = 3 * N_GROUPS
    head_blk = lambda idx: pl.BlockSpec((None, CHUNK, HEAD_DIM), lambda bi, h, c: (bi, c, idx * HEADS + h))
    carry_rows = [BAND * dil for dil in DILATIONS]
    return pl.pallas_call(
        _attn_kernel,
        out_shape=jax.ShapeDtypeStruct((b, s, aw), BF16),
        grid=(b, HEADS, s // CHUNK),
        in_specs=[head_blk(idx) for idx in range(n_in)],
        out_specs=head_blk(0),
        scratch_shapes=(
            [pltpu.VMEM((n, HEAD_DIM), BF16) for n in carry_rows] * 2
            + [pltpu.VMEM((CHUNK, HEAD_DIM), F32)] * (2 * N_GROUPS)),
        compiler_params=_params(3),
        name="attn",
    )(*([qkv] * n_in))


def _mixout_kernel(x_ref, mod_ref, o_ref, yc_ref, sga_ref, sgc_ref, wa_ref, wc_ref, wo_ref, out_ref):
    for r0 in range(0, x_ref.shape[0], MIX_SUB_ROWS):
        rows = slice(r0, r0 + MIX_SUB_ROWS)
        y_attn = _dot(o_ref[rows, :], wa_ref[...])
        y_conv = _dot(yc_ref[rows, :], wc_ref[...])
        merged = (sga_ref[rows, :].astype(F32) * y_attn + sgc_ref[rows, :].astype(F32) * y_conv).astype(BF16)
        out_ref[rows, :] = x_ref[rows, :] + _mod_row(mod_ref, 5) * _dot(merged, wo_ref[...])


def _mixout(x, mod, o, mix, w_attn, w_conv, w_out, tm=1024):
    b, s, d = x.shape
    row = lambda width, col=0: pl.BlockSpec((None, tm, width), lambda bi, i: (bi, i, col))
    return pl.pallas_call(
        _mixout_kernel,
        out_shape=jax.ShapeDtypeStruct(x.shape, F32),
        grid=(b, s // tm),
        in_specs=[row(d), _mod_spec(d), row(o.shape[-1]),
                  row(d, 0), row(d, 1), row(d, 2),
                  _resident(w_attn.shape), _resident(w_conv.shape), _resident(w_out.shape)],
        out_specs=row(d),
        compiler_params=_params(2),
        name="mixout",
    )(x, mod, o, mix, mix, mix, w_attn, w_conv, w_out)


def kernel(x, c, w_ada, b_ada, norm_ffn1, ffn1_w_gate, ffn1_w_up, ffn1_w_down, norm_mix, w_in, q_norm, k_norm,
           conv_w, w_attn_branch, w_conv_branch, w_out, norm_ffn2, ffn2_w_gate, ffn2_w_up, ffn2_w_down):
    b, s, d = x.shape
    depth = w_ada.shape[0]
    for l in range(depth):
        mod, w_gate1, w_up1, w_down1 = _adaln(c, w_ada, b_ada, l,
                                              cast_ws=(ffn1_w_gate[l], ffn1_w_up[l], ffn1_w_down[l]))
        x, w_in_b = _ffn(x, mod, norm_ffn1[l][None, :], w_gate1, w_up1, w_down1, mod_row=0, cast_ws=(w_in[l],))
        qkv, mix, w_attn_b, w_conv_b, w_out_b, w_gate_b, w_up_b, w_down_b = _proj(
            x, mod, norm_mix[l][None, :], q_norm[l][None, :], k_norm[l][None, :], conv_w[l], w_in_b,
            cast_ws=(w_attn_branch[l], w_conv_branch[l], w_out[l], ffn2_w_gate[l], ffn2_w_up[l], ffn2_w_down[l]))
        o = _attn(qkv)
        x = _mixout(x, mod, o, mix, w_attn_b, w_conv_b, w_out_b)
        x, = _ffn(x, mod, norm_ffn2[l][None, :], w_gate_b, w_up_b, w_down_b, mod_row=6)
    return x
```

```python
import functools

import jax
import jax.numpy as jnp
from jax import lax
from jax.experimental import pallas as pl
from jax.experimental.pallas import tpu as pltpu

EPS = 1e-6
N_MOD = 9
N_GROUPS = 3
HEADS = 4
HEAD_DIM = 128
DILATIONS = (1, 4, 16)
BAND = 128
CHUNK = BAND * max(DILATIONS)
TILE = 512
FFN_SUB_ROWS = 256
MIX_SUB_ROWS = 256
CONV_K = 3
LOG2_E = 1.4426950408889634

V7X_VMEM_BYTES = 64 * 1024 * 1024
VMEM_LIMIT = V7X_VMEM_BYTES - 8 * 1024 * 1024

BF16 = jnp.bfloat16
BF16_SUBLANES = 16
F32_SUBLANES = 8
F32 = jnp.float32


def _resident(shape):
    return pl.BlockSpec(shape, lambda *_: (0,) * len(shape), pipeline_mode=pl.Buffered(1))


def _params(n_axes):
    return pltpu.CompilerParams(dimension_semantics=("arbitrary",) * n_axes, vmem_limit_bytes=VMEM_LIMIT)


def _rms_mod(x, gain, shift, scale):
    ms = jnp.mean(x * x, axis=-1, keepdims=True)
    return x * lax.rsqrt(ms + EPS) * (gain * (1.0 + scale)) + shift


def _dot(a, b):
    return jnp.dot(a, b, preferred_element_type=F32)


def _cast_specs(ws, grid):
    n_steps = 1
    for g in grid:
        n_steps *= g
    specs, shapes = [], []
    for w in ws:
        rows, cols = w.shape
        n_bands = 1
        while n_bands * 2 <= n_steps:
            n_bands *= 2
        while rows % n_bands or (rows // n_bands) % BF16_SUBLANES:
            n_bands //= 2
        per = n_steps // n_bands

        def index(*ids, per=per, last=n_bands - 1):
            flat = ids[0]
            for i, g in zip(ids[1:], grid[1:]):
                flat = flat * g + i
            return (jnp.minimum(flat // per, last), 0)

        specs.append(pl.BlockSpec((rows // n_bands, cols), index))
        shapes.append(jax.ShapeDtypeStruct(w.shape, BF16))
    return specs, shapes


def _cast_bands(src_refs, dst_refs):
    for src, dst in zip(src_refs, dst_refs):
        dst[...] = src[...].astype(BF16)


def _adaln_kernel(n_cast, c_ref, w_ref, b_ref, *rest):
    o_ref = rest[n_cast]
    c = c_ref[...]
    c_act = c * jax.nn.sigmoid(c)
    lhs = jnp.concatenate([c_act, jnp.zeros((F32_SUBLANES - c.shape[0], c.shape[1]), F32)], axis=0)
    o_ref[...] = _dot(lhs, w_ref[...]) + b_ref[...]
    _cast_bands(rest[:n_cast], rest[n_cast + 1:])


def _adaln(c, w_ada, b_ada, layer, cast_ws=()):
    b, d = c.shape
    grid = (N_MOD,)
    cast_specs, cast_shapes = _cast_specs(cast_ws, grid)
    return pl.pallas_call(
        functools.partial(_adaln_kernel, len(cast_ws)),
        out_shape=[jax.ShapeDtypeStruct((N_MOD, F32_SUBLANES, d), F32)] + cast_shapes,
        grid=grid,
        in_specs=[
            pl.BlockSpec((b, d), lambda j: (0, 0)),
            pl.BlockSpec((None, d, d), lambda j: (layer, 0, j)),
            pl.BlockSpec((1, d), lambda j: (layer, j)),
        ] + cast_specs,
        out_specs=[pl.BlockSpec((None, F32_SUBLANES, d), lambda j: (j, 0, 0))] + cast_specs,
        compiler_params=_params(1),
        name="adaln",
    )(c, w_ada, b_ada, *cast_ws)


def _mod_row(mod_ref, k):
    return mod_ref[k, pl.ds(pl.program_id(0), 1), :]


def _mod_spec(d):
    return _resident((N_MOD, F32_SUBLANES, d))


def _ffn_kernel(mod_row, n_cast, x_ref, mod_ref, g_ref, wg_ref, wu_ref, wd_ref, *rest):
    o_ref = rest[n_cast]
    shift = _mod_row(mod_ref, mod_row)
    scale = _mod_row(mod_ref, mod_row + 1)
    gate = _mod_row(mod_ref, mod_row + 2)
    for r0 in range(0, x_ref.shape[0], FFN_SUB_ROWS):
        x = x_ref[r0:r0 + FFN_SUB_ROWS, :]
        h = _rms_mod(x, g_ref[...], shift, scale).astype(BF16)
        a = _dot(h, wg_ref[...])
        u = _dot(h, wu_ref[...])
        act = (a * jax.nn.sigmoid(a) * u).astype(BF16)
        y = _dot(act, wd_ref[...])
        o_ref[r0:r0 + FFN_SUB_ROWS, :] = x + 0.5 * gate * y
    _cast_bands(rest[:n_cast], rest[n_cast + 1:])


def _ffn(x, mod, gain, w_gate, w_up, w_down, mod_row, cast_ws=(), tm=1024):
    b, s, d = x.shape
    f = w_gate.shape[1]
    grid = (b, s // tm)
    cast_specs, cast_shapes = _cast_specs(cast_ws, grid)
    row = pl.BlockSpec((None, tm, d), lambda bi, i: (bi, i, 0))
    return pl.pallas_call(
        functools.partial(_ffn_kernel, mod_row, len(cast_ws)),
        out_shape=[jax.ShapeDtypeStruct(x.shape, F32)] + cast_shapes,
        grid=grid,
        in_specs=[
            row,
            _mod_spec(d),
            _resident((1, d)),
            _resident((d, f)),
            _resident((d, f)),
            _resident((f, d)),
        ] + cast_specs,
        out_specs=[row] + cast_specs,
        compiler_params=_params(2),
        name="ffn",
    )(x, mod, gain, w_gate, w_up, w_down, *cast_ws)


def _regroup_rows(src, dst_f32, dst_bf16, seg):
    q = seg // 4
    for base in range(0, TILE, seg):
        for r in range(4):
            lo = base + r * q
            for j in range(src.shape[0]):
                piece = src[j, pl.ds(base + r, q, stride=4), :]
                if dst_f32 is not None:
                    dst_f32[j, lo:lo + q, :] = piece
                dst_bf16[lo:lo + q, j * HEAD_DIM:(j + 1) * HEAD_DIM] = piece.astype(BF16)


def _proj_kernel(n_cast, x_ref, mod_ref, g_ref, qn_ref, kn_ref, cw_ref, w_ref, *rest):
    cast_in, rest = rest[:n_cast], rest[n_cast:]
    qkv_ref, mix_ref = rest[:2]
    cast_out = rest[2:2 + n_cast]
    scr_a, scr_b, h_scr, xc_scr = rest[2 + n_cast:]

    @pl.when(pl.program_id(1) == 0)
    def _():
        xc_scr[:F32_SUBLANES, :] = jnp.zeros((F32_SUBLANES, xc_scr.shape[1]), xc_scr.dtype)

    d = g_ref.shape[-1]
    aw = HEADS * HEAD_DIM
    hf = _rms_mod(x_ref[...], g_ref[...], _mod_row(mod_ref, 3), _mod_row(mod_ref, 4))
    h = hf.astype(BF16)
    for j in range(d // HEAD_DIM):
        scr_a[j] = hf[:, j * HEAD_DIM:(j + 1) * HEAD_DIM]
    _regroup_rows(scr_a, scr_b, h_scr.at[0], TILE)
    _regroup_rows(scr_b, None, h_scr.at[1], TILE // 4)
    o = 3 * N_GROUPS * aw
    mix_ref[:, d:2 * d] = jax.nn.sigmoid(_dot(h, w_ref[:, o + 3 * d:o + 4 * d])).astype(BF16)
    mix_ref[:, 2 * d:] = jax.nn.sigmoid(_dot(h, w_ref[:, o + 4 * d:o + 5 * d])).astype(BF16)
    pad = xc_scr.shape[0] - TILE
    xc = _dot(h, w_ref[:, o + 2 * d:o + 3 * d]) * _dot(h, w_ref[:, o:o + d])
    xc_scr[pad:, :] = xc
    conv = xc * cw_ref[CONV_K - 1:CONV_K, :]
    for j in range(CONV_K - 1):
        back = CONV_K - 1 - j
        conv = conv + xc_scr[pad - back:pad - back + TILE, :] * cw_ref[j:j + 1, :]
    mix_ref[:, :d] = (_dot(h, w_ref[:, o + d:o + 2 * d]) * conv).astype(BF16)
    xc_scr[:pad, :] = xc_scr[TILE:, :]
    gains = (qn_ref[...] * (HEAD_DIM ** -0.5 * LOG2_E), kn_ref[...], None)
    for j in range(3):
        for g in range(N_GROUPS):
            c0 = (j * N_GROUPS + g) * aw
            t = _dot(h if g == 0 else h_scr[g - 1], w_ref[:, c0:c0 + aw])
            for hd in range(HEADS):
                col = hd * HEAD_DIM
                slab = t[:, col:col + HEAD_DIM]
                if gains[j] is not None:
                    ms = jnp.mean(slab * slab, axis=-1, keepdims=True)
                    slab = slab * lax.rsqrt(ms + EPS) * gains[j]
                dst = (3 * g + j) * aw + col
                qkv_ref[:, dst:dst + HEAD_DIM] = slab.astype(BF16)
    _cast_bands(cast_in, cast_out)


def _proj(x, mod, gain, q_norm, k_norm, conv_w, w, cast_ws=()):
    b, s, d = x.shape
    aw = HEADS * HEAD_DIM
    grid = (b, s // TILE)
    cast_specs, cast_shapes = _cast_specs(cast_ws, grid)
    row = lambda width: pl.BlockSpec((None, TILE, width), lambda bi, i: (bi, i, 0))
    qkv_w, mix_w = 3 * N_GROUPS * aw, 3 * d
    return pl.pallas_call(
        functools.partial(_proj_kernel, len(cast_ws)),
        out_shape=[jax.ShapeDtypeStruct((b, s, qkv_w), BF16), jax.ShapeDtypeStruct((b, s, mix_w), BF16)] + cast_shapes,
        grid=grid,
        in_specs=[
            row(d),
            _mod_spec(d),
            _resident((1, d)),
            _resident((1, HEAD_DIM)),
            _resident((1, HEAD_DIM)),
            _resident(conv_w.shape),
            _resident(w.shape),
        ] + cast_specs,
        out_specs=[row(qkv_w), row(mix_w)] + cast_specs,
        scratch_shapes=[pltpu.VMEM((d // HEAD_DIM, TILE, HEAD_DIM), F32), pltpu.VMEM((d // HEAD_DIM, TILE, HEAD_DIM), F32),
                        pltpu.VMEM((2, TILE, d), BF16), pltpu.VMEM((F32_SUBLANES + TILE, d), F32)],
        compiler_params=_params(2),
        name="proj",
    )(x, mod, gain, q_norm, k_norm, conv_w, w, *cast_ws)


def _attn_block(q, kcat, vcat, bias):
    s = lax.dot_general(q, kcat, (((1,), (1,)), ((), ())), preferred_element_type=F32) + bias
    m = jnp.max(s, axis=-1, keepdims=True)
    p = jnp.exp2(s - m)
    od = _dot(p.astype(BF16), jnp.concatenate([vcat, jnp.ones_like(vcat)], axis=1))
    return od[:, :HEAD_DIM], jnp.broadcast_to(m, (BAND, HEAD_DIM)), od[:, HEAD_DIM:]


def _band_rows(dil, r, jb):
    per_tile = TILE // dil
    if dil == 1:
        return [(jb * BAND, BAND)]
    pos = r if dil == 4 else (r % 4) * 4 + r // 4
    tiles = BAND // per_tile
    return [((jb * tiles + t) * TILE + pos * per_tile, per_tile) for t in range(tiles)]


def _gather(ref, ranges):
    parts = [ref[a:a + n, :] for a, n in ranges]
    return parts[0] if len(parts) == 1 else jnp.concatenate(parts, axis=0)


def _attn_kernel(*refs):
    ins = refs[:3 * N_GROUPS]
    out_ref = refs[3 * N_GROUPS]
    scr = refs[3 * N_GROUPS + 1:]
    kprev, vprev = scr[0:N_GROUPS], scr[N_GROUPS:2 * N_GROUPS]
    pieces = scr[2 * N_GROUPS:]
    nat = (None,) + tuple(pieces[3 * (g - 1):3 * g] for g in range(1, N_GROUPS))
    tmp = pieces[3 * (N_GROUPS - 1):]
    c = pl.program_id(2)

    @pl.when(c == 0)
    def _():
        for ref in kprev + vprev:
            ref[...] = jnp.zeros(ref.shape, ref.dtype)

    qi = lax.broadcasted_iota(jnp.int32, (BAND, 2 * BAND), 0)
    kj = lax.broadcasted_iota(jnp.int32, (BAND, 2 * BAND), 1)
    in_prev = (kj < BAND) & (kj >= qi)
    in_cur = (kj >= BAND) & (kj - BAND <= qi)
    neg = jnp.float32(-jnp.inf)
    bias_in = jnp.where(in_prev | in_cur, 0.0, neg)
    bias_first = jnp.where(in_cur, 0.0, neg)
    bias_edge = jnp.where(c > 0, bias_in, bias_first)

    for g in tuple(range(1, N_GROUPS)) + (0,):
        dil = DILATIONS[g]
        q_ref, k_ref, v_ref = ins[3 * g:3 * g + 3]
        blocks = CHUNK // dil // BAND
        carry = [rg for r in range(dil) for rg in _band_rows(dil, r, blocks - 1)]
        for r in range(dil):
            for jb in range(blocks):
                cur = _band_rows(dil, r, jb)
                q = _gather(q_ref, cur)
                if jb == 0:
                    base = r * BAND
                    k_prev = kprev[g][base:base + BAND, :]
                    v_prev = vprev[g][base:base + BAND, :]
                    bias = bias_edge
                else:
                    prev = _band_rows(dil, r, jb - 1)
                    k_prev, v_prev = _gather(k_ref, prev), _gather(v_ref, prev)
                    bias = bias_in
                kcat = jnp.concatenate([k_prev, _gather(k_ref, cur)], axis=0)
                vcat = jnp.concatenate([v_prev, _gather(v_ref, cur)], axis=0)
                block = _attn_block(q, kcat, vcat, bias)
                if dil == 16:
                    rows = pl.ds((r % 4) * (CHUNK // 4) + r // 4, BAND, stride=4)
                    for ref, piece in zip(tmp, block):
                        ref[rows, :] = piece
                elif dil > 1:
                    rows = pl.ds(jb * BAND * dil + r, BAND, stride=dil)
                    for ref, piece in zip(nat[g], block):
                        ref[rows, :] = piece
                else:
                    a = jb * BAND
                    groups = [block] + [[ref[a:a + BAND, :] for ref in nat[gg]] for gg in range(1, N_GROUPS)]
                    m = functools.reduce(jnp.maximum, [mg for _, mg, _ in groups])
                    es = [jnp.exp2(mg - m) for _, mg, _ in groups]
                    num = sum(e * og for e, (og, _, _) in zip(es, groups))
                    den = sum(e * dg for e, (_, _, dg) in zip(es, groups))
                    out_ref[a:a + BAND, :] = (num / den).astype(BF16)
        if dil == 16:
            seg = CHUNK // 4
            for r4 in range(4):
                for dst, src in zip(nat[g], tmp):
                    dst[pl.ds(r4, seg, stride=4), :] = src[r4 * seg:(r4 + 1) * seg, :]
        off = 0
        for a, n in carry:
            kprev[g][off:off + n, :] = k_ref[a:a + n, :]
            vprev[g][off:off + n, :] = v_ref[a:a + n, :]
            off += n


def _attn(qkv):
    b, s, _ = qkv.shape
    aw = HEADS * HEAD_DIM
    n_in = 3 * N_GROUPS
    head_blk = lambda idx: pl.BlockSpec((None, CHUNK, HEAD_DIM), lambda bi, h, c: (bi, c, idx * HEADS + h))
    carry_rows = [BAND * dil for dil in DILATIONS]
    return pl.pallas_call(
        _attn_kernel,
        out_shape=jax.ShapeDtypeStruct((b, s, aw), BF16),
        grid=(b, HEADS, s // CHUNK),
        in_specs=[head_blk(idx) for idx in range(n_in)],
        out_specs=head_blk(0),
        scratch_shapes=(
            [pltpu.VMEM((n, HEAD_DIM), BF16) for n in carry_rows] * 2
            + [pltpu.VMEM((CHUNK, HEAD_DIM), F32)] * (3 * N_GROUPS)),
        compiler_params=_params(3),
        name="attn",
    )(*([qkv] * n_in))


def _mixout_kernel(x_ref, mod_ref, o_ref, yc_ref, sga_ref, sgc_ref, wa_ref, wc_ref, wo_ref, out_ref):
    for r0 in range(0, x_ref.shape[0], MIX_SUB_ROWS):
        rows = slice(r0, r0 + MIX_SUB_ROWS)
        y_attn = _dot(o_ref[rows, :], wa_ref[...])
        y_conv = _dot(yc_ref[rows, :], wc_ref[...])
        merged = (sga_ref[rows, :].astype(F32) * y_attn + sgc_ref[rows, :].astype(F32) * y_conv).astype(BF16)
        out_ref[rows, :] = x_ref[rows, :] + _mod_row(mod_ref, 5) * _dot(merged, wo_ref[...])


def _mixout(x, mod, o, mix, w_attn, w_conv, w_out, tm=1024):
    b, s, d = x.shape
    row = lambda width, col=0: pl.BlockSpec((None, tm, width), lambda bi, i: (bi, i, col))
    return pl.pallas_call(
        _mixout_kernel,
        out_shape=jax.ShapeDtypeStruct(x.shape, F32),
        grid=(b, s // tm),
        in_specs=[row(d), _mod_spec(d), row(o.shape[-1]),
                  row(d, 0), row(d, 1), row(d, 2),
                  _resident(w_attn.shape), _resident(w_conv.shape), _resident(w_out.shape)],
        out_specs=row(d),
        compiler_params=_params(2),
        name="mixout",
    )(x, mod, o, mix, mix, mix, w_attn, w_conv, w_out)


def kernel(x, c, w_ada, b_ada, norm_ffn1, ffn1_w_gate, ffn1_w_up, ffn1_w_down, norm_mix, w_in, q_norm, k_norm,
           conv_w, w_attn_branch, w_conv_branch, w_out, norm_ffn2, ffn2_w_gate, ffn2_w_up, ffn2_w_down):
    b, s, d = x.shape
    depth = w_ada.shape[0]
    for l in range(depth):
        mod, w_gate1, w_up1, w_down1 = _adaln(c, w_ada, b_ada, l,
                                              cast_ws=(ffn1_w_gate[l], ffn1_w_up[l], ffn1_w_down[l]))
        x, w_in_b = _ffn(x, mod, norm_ffn1[l][None, :], w_gate1, w_up1, w_down1, mod_row=0, cast_ws=(w_in[l],))
        qkv, mix, w_attn_b, w_conv_b, w_out_b, w_gate_b, w_up_b, w_down_b = _proj(
            x, mod, norm_mix[l][None, :], q_norm[l][None, :], k_norm[l][None, :], conv_w[l], w_in_b,
            cast_ws=(w_attn_branch[l], w_conv_branch[l], w_out[l], ffn2_w_gate[l], ffn2_w_up[l], ffn2_w_down[l]))
        o = _attn(qkv)
        x = _mixout(x, mod, o, mix, w_attn_b, w_conv_b, w_out_b)
        x, = _ffn(x, mod, norm_ffn2[l][None, :], w_gate_b, w_up_b, w_down_b, mod_row=6)
    return x
```

```python
import functools

import jax
import jax.numpy as jnp
from jax import lax
from jax.experimental import pallas as pl
from jax.experimental.pallas import tpu as pltpu

EPS = 1e-6
N_MOD = 9
N_GROUPS = 3
HEADS = 4
HEAD_DIM = 128
DILATIONS = (1, 4, 16)
BAND = 128
CHUNK = BAND * max(DILATIONS)
TILE = 512
FFN_SUB_ROWS = 256
MIX_SUB_ROWS = 256
PROJ_SUB_ROWS = 256
CONV_K = 3
LOG2_E = 1.4426950408889634

V7X_VMEM_BYTES = 64 * 1024 * 1024
VMEM_LIMIT = V7X_VMEM_BYTES - 8 * 1024 * 1024

BF16 = jnp.bfloat16
BF16_SUBLANES = 16
F32_SUBLANES = 8
F32 = jnp.float32


def _resident(shape):
    return pl.BlockSpec(shape, lambda *_: (0,) * len(shape), pipeline_mode=pl.Buffered(1))


def _params(n_axes):
    return pltpu.CompilerParams(dimension_semantics=("arbitrary",) * n_axes, vmem_limit_bytes=VMEM_LIMIT)


def _rms_mod(x, gain, shift, scale):
    ms = jnp.mean(x * x, axis=-1, keepdims=True)
    return x * lax.rsqrt(ms + EPS) * (gain * (1.0 + scale)) + shift


def _dot(a, b):
    return jnp.dot(a, b, preferred_element_type=F32)


def _cast_specs(ws, grid):
    n_steps = 1
    for g in grid:
        n_steps *= g
    specs, shapes = [], []
    for w in ws:
        rows, cols = w.shape
        n_bands = 1
        while n_bands * 2 <= n_steps:
            n_bands *= 2
        while rows % n_bands or (rows // n_bands) % BF16_SUBLANES:
            n_bands //= 2
        per = n_steps // n_bands

        def index(*ids, per=per, last=n_bands - 1):
            flat = ids[0]
            for i, g in zip(ids[1:], grid[1:]):
                flat = flat * g + i
            return (jnp.minimum(flat // per, last), 0)

        specs.append(pl.BlockSpec((rows // n_bands, cols), index))
        shapes.append(jax.ShapeDtypeStruct(w.shape, BF16))
    return specs, shapes


def _cast_bands(src_refs, dst_refs):
    for src, dst in zip(src_refs, dst_refs):
        dst[...] = src[...].astype(BF16)


def _adaln_kernel(n_cast, c_ref, w_ref, b_ref, *rest):
    o_ref = rest[n_cast]
    c = c_ref[...]
    c_act = c * jax.nn.sigmoid(c)
    lhs = jnp.concatenate([c_act, jnp.zeros((F32_SUBLANES - c.shape[0], c.shape[1]), F32)], axis=0)
    o_ref[...] = _dot(lhs, w_ref[...]) + b_ref[...]
    _cast_bands(rest[:n_cast], rest[n_cast + 1:])


def _adaln(c, w_ada, b_ada, layer, cast_ws=()):
    b, d = c.shape
    grid = (N_MOD,)
    cast_specs, cast_shapes = _cast_specs(cast_ws, grid)
    return pl.pallas_call(
        functools.partial(_adaln_kernel, len(cast_ws)),
        out_shape=[jax.ShapeDtypeStruct((N_MOD, F32_SUBLANES, d), F32)] + cast_shapes,
        grid=grid,
        in_specs=[
            pl.BlockSpec((b, d), lambda j: (0, 0)),
            pl.BlockSpec((None, d, d), lambda j: (layer, 0, j)),
            pl.BlockSpec((1, d), lambda j: (layer, j)),
        ] + cast_specs,
        out_specs=[pl.BlockSpec((None, F32_SUBLANES, d), lambda j: (j, 0, 0))] + cast_specs,
        compiler_params=_params(1),
        name="adaln",
    )(c, w_ada, b_ada, *cast_ws)


def _mod_row(mod_ref, k):
    return mod_ref[k, pl.ds(pl.program_id(0), 1), :]


def _mod_spec(d):
    return _resident((N_MOD, F32_SUBLANES, d))


def _ffn_kernel(mod_row, n_cast, x_ref, mod_ref, g_ref, wg_ref, wu_ref, wd_ref, *rest):
    o_ref = rest[n_cast]
    shift = _mod_row(mod_ref, mod_row)
    scale = _mod_row(mod_ref, mod_row + 1)
    gate = _mod_row(mod_ref, mod_row + 2)
    for r0 in range(0, x_ref.shape[0], FFN_SUB_ROWS):
        x = x_ref[r0:r0 + FFN_SUB_ROWS, :]
        h = _rms_mod(x, g_ref[...], shift, scale).astype(BF16)
        a = _dot(h, wg_ref[...])
        u = _dot(h, wu_ref[...])
        act = (a * jax.nn.sigmoid(a) * u).astype(BF16)
        y = _dot(act, wd_ref[...])
        o_ref[r0:r0 + FFN_SUB_ROWS, :] = x + 0.5 * gate * y
    _cast_bands(rest[:n_cast], rest[n_cast + 1:])


def _ffn(x, mod, gain, w_gate, w_up, w_down, mod_row, cast_ws=(), tm=1024):
    b, s, d = x.shape
    f = w_gate.shape[1]
    grid = (b, s // tm)
    cast_specs, cast_shapes = _cast_specs(cast_ws, grid)
    row = pl.BlockSpec((None, tm, d), lambda bi, i: (bi, i, 0))
    return pl.pallas_call(
        functools.partial(_ffn_kernel, mod_row, len(cast_ws)),
        out_shape=[jax.ShapeDtypeStruct(x.shape, F32)] + cast_shapes,
        grid=grid,
        in_specs=[
            row,
            _mod_spec(d),
            _resident((1, d)),
            _resident((d, f)),
            _resident((d, f)),
            _resident((f, d)),
        ] + cast_specs,
        out_specs=[row] + cast_specs,
        compiler_params=_params(2),
        name="ffn",
    )(x, mod, gain, w_gate, w_up, w_down, *cast_ws)


def _regroup_rows(src, dst_f32, dst_bf16, seg):
    q = seg // 4
    for base in range(0, TILE, seg):
        for r in range(4):
            lo = base + r * q
            for j in range(src.shape[0]):
                piece = src[j, pl.ds(base + r, q, stride=4), :]
                if dst_f32 is not None:
                    dst_f32[j, lo:lo + q, :] = piece
                dst_bf16[lo:lo + q, j * HEAD_DIM:(j + 1) * HEAD_DIM] = piece.astype(BF16)


def _proj_kernel(n_cast, x_ref, mod_ref, g_ref, qn_ref, kn_ref, cw_ref, w_ref, *rest):
    cast_in, rest = rest[:n_cast], rest[n_cast:]
    qkv_ref, mix_ref = rest[:2]
    cast_out = rest[2:2 + n_cast]
    scr_a, scr_b, h_scr, xc_scr = rest[2 + n_cast:]

    @pl.when(pl.program_id(1) == 0)
    def _():
        xc_scr[:F32_SUBLANES, :] = jnp.zeros((F32_SUBLANES, xc_scr.shape[1]), xc_scr.dtype)

    d = g_ref.shape[-1]
    aw = HEADS * HEAD_DIM
    subs = tuple(range(0, TILE, PROJ_SUB_ROWS))
    hs = []
    for r0 in subs:
        hf = _rms_mod(x_ref[r0:r0 + PROJ_SUB_ROWS, :], g_ref[...], _mod_row(mod_ref, 3), _mod_row(mod_ref, 4))
        hs.append(hf.astype(BF16))
        for j in range(d // HEAD_DIM):
            scr_a[j, r0:r0 + PROJ_SUB_ROWS, :] = hf[:, j * HEAD_DIM:(j + 1) * HEAD_DIM]
    _regroup_rows(scr_a, scr_b, h_scr.at[0], TILE)
    _regroup_rows(scr_b, None, h_scr.at[1], TILE // 4)
    o = 3 * N_GROUPS * aw
    pad = xc_scr.shape[0] - TILE
    for r0, h in zip(subs, hs):
        rows = slice(r0, r0 + PROJ_SUB_ROWS)
        mix_ref[rows, d:2 * d] = jax.nn.sigmoid(_dot(h, w_ref[:, o + 3 * d:o + 4 * d])).astype(BF16)
        mix_ref[rows, 2 * d:] = jax.nn.sigmoid(_dot(h, w_ref[:, o + 4 * d:o + 5 * d])).astype(BF16)
        xc = _dot(h, w_ref[:, o + 2 * d:o + 3 * d]) * _dot(h, w_ref[:, o:o + d])
        xc_scr[pad + r0:pad + r0 + PROJ_SUB_ROWS, :] = xc
        conv = xc * cw_ref[CONV_K - 1:CONV_K, :]
        for j in range(CONV_K - 1):
            back = CONV_K - 1 - j
            conv = conv + xc_scr[pad + r0 - back:pad + r0 - back + PROJ_SUB_ROWS, :] * cw_ref[j:j + 1, :]
        mix_ref[rows, :d] = (_dot(h, w_ref[:, o + d:o + 2 * d]) * conv).astype(BF16)
    xc_scr[:pad, :] = xc_scr[TILE:, :]
    gains = (qn_ref[...] * (HEAD_DIM ** -0.5 * LOG2_E), kn_ref[...], None)
    for j in range(3):
        for g in range(N_GROUPS):
            c0 = (j * N_GROUPS + g) * aw
            for r0, h in zip(subs, hs):
                lhs = h if g == 0 else h_scr[g - 1, r0:r0 + PROJ_SUB_ROWS, :]
                t = _dot(lhs, w_ref[:, c0:c0 + aw])
                for hd in range(HEADS):
                    col = hd * HEAD_DIM
                    slab = t[:, col:col + HEAD_DIM]
                    if gains[j] is not None:
                        ms = jnp.mean(slab * slab, axis=-1, keepdims=True)
                        slab = slab * lax.rsqrt(ms + EPS) * gains[j]
                    dst = (3 * g + j) * aw + col
                    qkv_ref[r0:r0 + PROJ_SUB_ROWS, dst:dst + HEAD_DIM] = slab.astype(BF16)
    _cast_bands(cast_in, cast_out)


def _proj(x, mod, gain, q_norm, k_norm, conv_w, w, cast_ws=()):
    b, s, d = x.shape
    aw = HEADS * HEAD_DIM
    grid = (b, s // TILE)
    cast_specs, cast_shapes = _cast_specs(cast_ws, grid)
    row = lambda width: pl.BlockSpec((None, TILE, width), lambda bi, i: (bi, i, 0))
    qkv_w, mix_w = 3 * N_GROUPS * aw, 3 * d
    return pl.pallas_call(
        functools.partial(_proj_kernel, len(cast_ws)),
        out_shape=[jax.ShapeDtypeStruct((b, s, qkv_w), BF16), jax.ShapeDtypeStruct((b, s, mix_w), BF16)] + cast_shapes,
        grid=grid,
        in_specs=[
            row(d),
            _mod_spec(d),
            _resident((1, d)),
            _resident((1, HEAD_DIM)),
            _resident((1, HEAD_DIM)),
            _resident(conv_w.shape),
            _resident(w.shape),
        ] + cast_specs,
        out_specs=[row(qkv_w), row(mix_w)] + cast_specs,
        scratch_shapes=[pltpu.VMEM((d // HEAD_DIM, TILE, HEAD_DIM), F32), pltpu.VMEM((d // HEAD_DIM, TILE, HEAD_DIM), F32),
                        pltpu.VMEM((2, TILE, d), BF16), pltpu.VMEM((F32_SUBLANES + TILE, d), F32)],
        compiler_params=_params(2),
        name="proj",
    )(x, mod, gain, q_norm, k_norm, conv_w, w, *cast_ws)


def _attn_block(q, kcat, vcat, bias):
    s = lax.dot_general(q, kcat, (((1,), (1,)), ((), ())), preferred_element_type=F32) + bias
    m = jnp.max(s, axis=-1, keepdims=True)
    p = jnp.exp2(s - m)
    od = _dot(p.astype(BF16), jnp.concatenate([vcat, jnp.ones_like(vcat)], axis=1))
    return od[:, :HEAD_DIM], jnp.broadcast_to(m, (BAND, HEAD_DIM)), od[:, HEAD_DIM:]


def _band_rows(dil, r, jb):
    per_tile = TILE // dil
    if dil == 1:
        return [(jb * BAND, BAND)]
    pos = r if dil == 4 else (r % 4) * 4 + r // 4
    tiles = BAND // per_tile
    return [((jb * tiles + t) * TILE + pos * per_tile, per_tile) for t in range(tiles)]


def _gather(ref, ranges):
    parts = [ref[a:a + n, :] for a, n in ranges]
    return parts[0] if len(parts) == 1 else jnp.concatenate(parts, axis=0)


def _attn_kernel(*refs):
    ins = refs[:3 * N_GROUPS]
    out_ref = refs[3 * N_GROUPS]
    scr = refs[3 * N_GROUPS + 1:]
    kprev, vprev = scr[0:N_GROUPS], scr[N_GROUPS:2 * N_GROUPS]
    pieces = scr[2 * N_GROUPS:]
    nat = (None,) + tuple(pieces[3 * (g - 1):3 * g] for g in range(1, N_GROUPS))
    tmp = pieces[3 * (N_GROUPS - 1):]
    c = pl.program_id(2)

    @pl.when(c == 0)
    def _():
        for ref in kprev + vprev:
            ref[...] = jnp.zeros(ref.shape, ref.dtype)

    qi = lax.broadcasted_iota(jnp.int32, (BAND, 2 * BAND), 0)
    kj = lax.broadcasted_iota(jnp.int32, (BAND, 2 * BAND), 1)
    in_prev = (kj < BAND) & (kj >= qi)
    in_cur = (kj >= BAND) & (kj - BAND <= qi)
    neg = jnp.float32(-jnp.inf)
    bias_in = jnp.where(in_prev | in_cur, 0.0, neg)
    bias_first = jnp.where(in_cur, 0.0, neg)
    bias_edge = jnp.where(c > 0, bias_in, bias_first)

    for g in tuple(range(1, N_GROUPS)) + (0,):
        dil = DILATIONS[g]
        q_ref, k_ref, v_ref = ins[3 * g:3 * g + 3]
        blocks = CHUNK // dil // BAND
        carry = [rg for r in range(dil) for rg in _band_rows(dil, r, blocks - 1)]
        for r in range(dil):
            for jb in range(blocks):
                cur = _band_rows(dil, r, jb)
                q = _gather(q_ref, cur)
                if jb == 0:
                    base = r * BAND
                    k_prev = kprev[g][base:base + BAND, :]
                    v_prev = vprev[g][base:base + BAND, :]
                    bias = bias_edge
                else:
                    prev = _band_rows(dil, r, jb - 1)
                    k_prev, v_prev = _gather(k_ref, prev), _gather(v_ref, prev)
                    bias = bias_in
                kcat = jnp.concatenate([k_prev, _gather(k_ref, cur)], axis=0)
                vcat = jnp.concatenate([v_prev, _gather(v_ref, cur)], axis=0)
                block = _attn_block(q, kcat, vcat, bias)
                if dil == 16:
                    rows = pl.ds((r % 4) * (CHUNK // 4) + r // 4, BAND, stride=4)
                    for ref, piece in zip(tmp, block):
                        ref[rows, :] = piece
                elif dil > 1:
                    rows = pl.ds(jb * BAND * dil + r, BAND, stride=dil)
                    for ref, piece in zip(nat[g], block):
                        ref[rows, :] = piece
                else:
                    a = jb * BAND
                    groups = [block] + [[ref[a:a + BAND, :] for ref in nat[gg]] for gg in range(1, N_GROUPS)]
                    m = functools.reduce(jnp.maximum, [mg for _, mg, _ in groups])
                    es = [jnp.exp2(mg - m) for _, mg, _ in groups]
                    num = sum(e * og for e, (og, _, _) in zip(es, groups))
                    den = sum(e * dg for e, (_, _, dg) in zip(es, groups))
                    out_ref[a:a + BAND, :] = (num / den).astype(BF16)
        if dil == 16:
            seg = CHUNK // 4
            for r4 in range(4):
                for dst, src in zip(nat[g], tmp):
                    dst[pl.ds(r4, seg, stride=4), :] = src[r4 * seg:(r4 + 1) * seg, :]
        off = 0
        for a, n in carry:
            kprev[g][off:off + n, :] = k_ref[a:a + n, :]
            vprev[g][off:off + n, :] = v_ref[a:a + n, :]
            off += n


def _attn(qkv):
    b, s, _ = qkv.shape
    aw = HEADS * HEAD_DIM
    n_in = 3 * N_GROUPS
    head_blk = lambda idx: pl.BlockSpec((None, CHUNK, HEAD_DIM), lambda bi, h, c: (bi, c, idx * HEADS + h))
    carry_rows = [BAND * dil for dil in DILATIONS]
    return pl.pallas_call(
        _attn_kernel,
        out_shape=jax.ShapeDtypeStruct((b, s, aw), BF16),
        grid=(b, HEADS, s // CHUNK),
        in_specs=[head_blk(idx) for idx in range(n_in)],
        out_specs=head_blk(0),
        scratch_shapes=(
            [pltpu.VMEM((n, HEAD_DIM), BF16) for n in carry_rows] * 2
            + [pltpu.VMEM((CHUNK, HEAD_DIM), F32)] * (3 * N_GROUPS)),
        compiler_params=_params(3),
        name="attn",
    )(*([qkv] * n_in))


def _mixout_kernel(x_ref, mod_ref, o_ref, yc_ref, sga_ref, sgc_ref, wa_ref, wc_ref, wo_ref, out_ref):
    for r0 in range(0, x_ref.shape[0], MIX_SUB_ROWS):
        rows = slice(r0, r0 + MIX_SUB_ROWS)
        y_attn = _dot(o_ref[rows, :], wa_ref[...])
        y_conv = _dot(yc_ref[rows, :], wc_ref[...])
        merged = (sga_ref[rows, :].astype(F32) * y_attn + sgc_ref[rows, :].astype(F32) * y_conv).astype(BF16)
        out_ref[rows, :] = x_ref[rows, :] + _mod_row(mod_ref, 5) * _dot(merged, wo_ref[...])


def _mixout(x, mod, o, mix, w_attn, w_conv, w_out, tm=1024):
    b, s, d = x.shape
    row = lambda width, col=0: pl.BlockSpec((None, tm, width), lambda bi, i: (bi, i, col))
    return pl.pallas_call(
        _mixout_kernel,
        out_shape=jax.ShapeDtypeStruct(x.shape, F32),
        grid=(b, s // tm),
        in_specs=[row(d), _mod_spec(d), row(o.shape[-1]),
                  row(d, 0), row(d, 1), row(d, 2),
                  _resident(w_attn.shape), _resident(w_conv.shape), _resident(w_out.shape)],
        out_specs=row(d),
        compiler_params=_params(2),
        name="mixout",
    )(x, mod, o, mix, mix, mix, w_attn, w_conv, w_out)


def kernel(x, c, w_ada, b_ada, norm_ffn1, ffn1_w_gate, ffn1_w_up, ffn1_w_down, norm_mix, w_in, q_norm, k_norm,
           conv_w, w_attn_branch, w_conv_branch, w_out, norm_ffn2, ffn2_w_gate, ffn2_w_up, ffn2_w_down):
    b, s, d = x.shape
    depth = w_ada.shape[0]
    for l in range(depth):
        mod, w_gate1, w_up1, w_down1 = _adaln(c, w_ada, b_ada, l,
                                              cast_ws=(ffn1_w_gate[l], ffn1_w_up[l], ffn1_w_down[l]))
        x, w_in_b = _ffn(x, mod, norm_ffn1[l][None, :], w_gate1, w_up1, w_down1, mod_row=0, cast_ws=(w_in[l],))
        qkv, mix, w_attn_b, w_conv_b, w_out_b, w_gate_b, w_up_b, w_down_b = _proj(
            x, mod, norm_mix[l][None, :], q_norm[l][None, :], k_norm[l][None, :], conv_w[l], w_in_b,
            cast_ws=(w_attn_branch[l], w_conv_branch[l], w_out[l], ffn2_w_gate[l], ffn2_w_up[l], ffn2_w_down[l]))
        o = _attn(qkv)
        x = _mixout(x, mod, o, mix, w_attn_b, w_conv_b, w_out_b)
        x, = _ffn(x, mod, norm_ffn2[l][None, :], w_gate_b, w_up_b, w_down_b, mod_row=6)
    return x
```

```python
import functools

import jax
import jax.numpy as jnp
from jax import lax
from jax.experimental import pallas as pl
from jax.experimental.pallas import tpu as pltpu

EPS = 1e-6
N_MOD = 9
N_GROUPS = 3
HEADS = 4
HEAD_DIM = 128
DILATIONS = (1, 4, 16)
BAND = 128
CHUNK = BAND * max(DILATIONS)
TILE = 512
FFN_SUB_ROWS = 256
MIX_SUB_ROWS = 512
PROJ_SUB_ROWS = 256
CONV_K = 3
LOG2_E = 1.4426950408889634

V7X_VMEM_BYTES = 64 * 1024 * 1024
VMEM_LIMIT = V7X_VMEM_BYTES - 8 * 1024 * 1024

BF16 = jnp.bfloat16
BF16_SUBLANES = 16
F32_SUBLANES = 8
F32 = jnp.float32


def _resident(shape):
    return pl.BlockSpec(shape, lambda *_: (0,) * len(shape), pipeline_mode=pl.Buffered(1))


def _params(n_axes):
    return pltpu.CompilerParams(dimension_semantics=("arbitrary",) * n_axes, vmem_limit_bytes=VMEM_LIMIT)


def _rms_mod(x, gain, shift, scale):
    ms = jnp.mean(x * x, axis=-1, keepdims=True)
    return x * lax.rsqrt(ms + EPS) * (gain * (1.0 + scale)) + shift


def _dot(a, b):
    return jnp.dot(a, b, preferred_element_type=F32)


def _cast_specs(ws, grid):
    n_steps = 1
    for g in grid:
        n_steps *= g
    specs, shapes = [], []
    for w in ws:
        rows, cols = w.shape
        n_bands = 1
        while n_bands * 2 <= n_steps:
            n_bands *= 2
        while rows % n_bands or (rows // n_bands) % BF16_SUBLANES:
            n_bands //= 2
        per = n_steps // n_bands

        def index(*ids, per=per, last=n_bands - 1):
            flat = ids[0]
            for i, g in zip(ids[1:], grid[1:]):
                flat = flat * g + i
            return (jnp.minimum(flat // per, last), 0)

        specs.append(pl.BlockSpec((rows // n_bands, cols), index))
        shapes.append(jax.ShapeDtypeStruct(w.shape, BF16))
    return specs, shapes


def _cast_bands(src_refs, dst_refs):
    for src, dst in zip(src_refs, dst_refs):
        dst[...] = src[...].astype(BF16)


def _adaln_kernel(n_cast, c_ref, w_ref, b_ref, *rest):
    o_ref = rest[n_cast]
    c = c_ref[...]
    c_act = c * jax.nn.sigmoid(c)
    lhs = jnp.concatenate([c_act, jnp.zeros((F32_SUBLANES - c.shape[0], c.shape[1]), F32)], axis=0)
    o_ref[...] = _dot(lhs, w_ref[...]) + b_ref[...]
    _cast_bands(rest[:n_cast], rest[n_cast + 1:])


def _adaln(c, w_ada, b_ada, layer, cast_ws=()):
    b, d = c.shape
    grid = (N_MOD,)
    cast_specs, cast_shapes = _cast_specs(cast_ws, grid)
    return pl.pallas_call(
        functools.partial(_adaln_kernel, len(cast_ws)),
        out_shape=[jax.ShapeDtypeStruct((N_MOD, F32_SUBLANES, d), F32)] + cast_shapes,
        grid=grid,
        in_specs=[
            pl.BlockSpec((b, d), lambda j: (0, 0)),
            pl.BlockSpec((None, d, d), lambda j: (layer, 0, j)),
            pl.BlockSpec((1, d), lambda j: (layer, j)),
        ] + cast_specs,
        out_specs=[pl.BlockSpec((None, F32_SUBLANES, d), lambda j: (j, 0, 0))] + cast_specs,
        compiler_params=_params(1),
        name="adaln",
    )(c, w_ada, b_ada, *cast_ws)


def _mod_row(mod_ref, k):
    return mod_ref[k, pl.ds(pl.program_id(0), 1), :]


def _mod_spec(d):
    return _resident((N_MOD, F32_SUBLANES, d))


def _ffn_kernel(mod_row, n_cast, x_ref, mod_ref, g_ref, wg_ref, wu_ref, wd_ref, *rest):
    o_ref = rest[n_cast]
    shift = _mod_row(mod_ref, mod_row)
    scale = _mod_row(mod_ref, mod_row + 1)
    gate = _mod_row(mod_ref, mod_row + 2)
    for r0 in range(0, x_ref.shape[0], FFN_SUB_ROWS):
        x = x_ref[r0:r0 + FFN_SUB_ROWS, :]
        h = _rms_mod(x, g_ref[...], shift, scale).astype(BF16)
        a = _dot(h, wg_ref[...])
        u = _dot(h, wu_ref[...])
        act = (a * jax.nn.sigmoid(a) * u).astype(BF16)
        y = _dot(act, wd_ref[...])
        o_ref[r0:r0 + FFN_SUB_ROWS, :] = x + 0.5 * gate * y
    _cast_bands(rest[:n_cast], rest[n_cast + 1:])


def _ffn(x, mod, gain, w_gate, w_up, w_down, mod_row, cast_ws=(), tm=1024):
    b, s, d = x.shape
    f = w_gate.shape[1]
    grid = (b, s // tm)
    cast_specs, cast_shapes = _cast_specs(cast_ws, grid)
    row = pl.BlockSpec((None, tm, d), lambda bi, i: (bi, i, 0))
    return pl.pallas_call(
        functools.partial(_ffn_kernel, mod_row, len(cast_ws)),
        out_shape=[jax.ShapeDtypeStruct(x.shape, F32)] + cast_shapes,
        grid=grid,
        in_specs=[
            row,
            _mod_spec(d),
            _resident((1, d)),
            _resident((d, f)),
            _resident((d, f)),
            _resident((f, d)),
        ] + cast_specs,
        out_specs=[row] + cast_specs,
        compiler_params=_params(2),
        name="ffn",
    )(x, mod, gain, w_gate, w_up, w_down, *cast_ws)


def _regroup_rows(src, dst_f32, dst_bf16, seg):
    q = seg // 4
    for base in range(0, TILE, seg):
        for r in range(4):
            lo = base + r * q
            for j in range(src.shape[0]):
                piece = src[j, pl.ds(base + r, q, stride=4), :]
                if dst_f32 is not None:
                    dst_f32[j, lo:lo + q, :] = piece
                dst_bf16[lo:lo + q, j * HEAD_DIM:(j + 1) * HEAD_DIM] = piece.astype(BF16)


def _proj_kernel(n_cast, x_ref, mod_ref, g_ref, qn_ref, kn_ref, cw_ref, w_ref, *rest):
    cast_in, rest = rest[:n_cast], rest[n_cast:]
    qkv_ref, mix_ref = rest[:2]
    cast_out = rest[2:2 + n_cast]
    scr_a, scr_b, h_scr, xc_scr = rest[2 + n_cast:]

    @pl.when(pl.program_id(1) == 0)
    def _():
        xc_scr[:F32_SUBLANES, :] = jnp.zeros((F32_SUBLANES, xc_scr.shape[1]), xc_scr.dtype)

    d = g_ref.shape[-1]
    aw = HEADS * HEAD_DIM
    subs = tuple(range(0, TILE, PROJ_SUB_ROWS))
    hs = []
    for r0 in subs:
        hf = _rms_mod(x_ref[r0:r0 + PROJ_SUB_ROWS, :], g_ref[...], _mod_row(mod_ref, 3), _mod_row(mod_ref, 4))
        hs.append(hf.astype(BF16))
        for j in range(d // HEAD_DIM):
            scr_a[j, r0:r0 + PROJ_SUB_ROWS, :] = hf[:, j * HEAD_DIM:(j + 1) * HEAD_DIM]
    _regroup_rows(scr_a, scr_b, h_scr.at[0], TILE)
    _regroup_rows(scr_b, None, h_scr.at[1], TILE // 4)
    o = 3 * N_GROUPS * aw
    pad = xc_scr.shape[0] - TILE
    for r0, h in zip(subs, hs):
        rows = slice(r0, r0 + PROJ_SUB_ROWS)
        mix_ref[rows, d:2 * d] = jax.nn.sigmoid(_dot(h, w_ref[:, o + 3 * d:o + 4 * d])).astype(BF16)
        mix_ref[rows, 2 * d:] = jax.nn.sigmoid(_dot(h, w_ref[:, o + 4 * d:o + 5 * d])).astype(BF16)
        xc = _dot(h, w_ref[:, o + 2 * d:o + 3 * d]) * _dot(h, w_ref[:, o:o + d])
        xc_scr[pad + r0:pad + r0 + PROJ_SUB_ROWS, :] = xc
        conv = xc * cw_ref[CONV_K - 1:CONV_K, :]
        for j in range(CONV_K - 1):
            back = CONV_K - 1 - j
            conv = conv + xc_scr[pad + r0 - back:pad + r0 - back + PROJ_SUB_ROWS, :] * cw_ref[j:j + 1, :]
        mix_ref[rows, :d] = (_dot(h, w_ref[:, o + d:o + 2 * d]) * conv).astype(BF16)
    xc_scr[:pad, :] = xc_scr[TILE:, :]
    gains = (qn_ref[...] * (HEAD_DIM ** -0.5 * LOG2_E), kn_ref[...], None)
    for j in range(3):
        for g in range(N_GROUPS):
            c0 = (j * N_GROUPS + g) * aw
            for r0, h in zip(subs, hs):
                lhs = h if g == 0 else h_scr[g - 1, r0:r0 + PROJ_SUB_ROWS, :]
                t = _dot(lhs, w_ref[:, c0:c0 + aw])
                for hd in range(HEADS):
                    col = hd * HEAD_DIM
                    slab = t[:, col:col + HEAD_DIM]
                    if gains[j] is not None:
                        ms = jnp.mean(slab * slab, axis=-1, keepdims=True)
                        slab = slab * lax.rsqrt(ms + EPS) * gains[j]
                    dst = (3 * g + j) * aw + col
                    qkv_ref[r0:r0 + PROJ_SUB_ROWS, dst:dst + HEAD_DIM] = slab.astype(BF16)
    _cast_bands(cast_in, cast_out)


def _proj(x, mod, gain, q_norm, k_norm, conv_w, w, cast_ws=()):
    b, s, d = x.shape
    aw = HEADS * HEAD_DIM
    grid = (b, s // TILE)
    cast_specs, cast_shapes = _cast_specs(cast_ws, grid)
    row = lambda width: pl.BlockSpec((None, TILE, width), lambda bi, i: (bi, i, 0))
    qkv_w, mix_w = 3 * N_GROUPS * aw, 3 * d
    return pl.pallas_call(
        functools.partial(_proj_kernel, len(cast_ws)),
        out_shape=[jax.ShapeDtypeStruct((b, s, qkv_w), BF16), jax.ShapeDtypeStruct((b, s, mix_w), BF16)] + cast_shapes,
        grid=grid,
        in_specs=[
            row(d),
            _mod_spec(d),
            _resident((1, d)),
            _resident((1, HEAD_DIM)),
            _resident((1, HEAD_DIM)),
            _resident(conv_w.shape),
            _resident(w.shape),
        ] + cast_specs,
        out_specs=[row(qkv_w), row(mix_w)] + cast_specs,
        scratch_shapes=[pltpu.VMEM((d // HEAD_DIM, TILE, HEAD_DIM), F32), pltpu.VMEM((d // HEAD_DIM, TILE, HEAD_DIM), F32),
                        pltpu.VMEM((2, TILE, d), BF16), pltpu.VMEM((F32_SUBLANES + TILE, d), F32)],
        compiler_params=_params(2),
        name="proj",
    )(x, mod, gain, q_norm, k_norm, conv_w, w, *cast_ws)


def _attn_block(q, kcat, vcat, bias):
    s = lax.dot_general(q, kcat, (((1,), (1,)), ((), ())), preferred_element_type=F32) + bias
    m = jnp.max(s, axis=-1, keepdims=True)
    p = jnp.exp2(s - m)
    od = _dot(p.astype(BF16), jnp.concatenate([vcat, jnp.ones_like(vcat)], axis=1))
    return od[:, :HEAD_DIM], jnp.broadcast_to(m, (BAND, HEAD_DIM)), od[:, HEAD_DIM:]


def _band_rows(dil, r, jb):
    per_tile = TILE // dil
    if dil == 1:
        return [(jb * BAND, BAND)]
    pos = r if dil == 4 else (r % 4) * 4 + r // 4
    tiles = BAND // per_tile
    return [((jb * tiles + t) * TILE + pos * per_tile, per_tile) for t in range(tiles)]


def _gather(ref, ranges):
    parts = [ref[a:a + n, :] for a, n in ranges]
    return parts[0] if len(parts) == 1 else jnp.concatenate(parts, axis=0)


def _attn_kernel(*refs):
    ins = refs[:3 * N_GROUPS]
    out_ref = refs[3 * N_GROUPS]
    scr = refs[3 * N_GROUPS + 1:]
    kprev, vprev = scr[0:N_GROUPS], scr[N_GROUPS:2 * N_GROUPS]
    pieces = scr[2 * N_GROUPS:]
    nat = (None,) + tuple(pieces[3 * (g - 1):3 * g] for g in range(1, N_GROUPS))
    tmp = pieces[3 * (N_GROUPS - 1):]
    c = pl.program_id(2)

    @pl.when(c == 0)
    def _():
        for ref in kprev + vprev:
            ref[...] = jnp.zeros(ref.shape, ref.dtype)

    qi = lax.broadcasted_iota(jnp.int32, (BAND, 2 * BAND), 0)
    kj = lax.broadcasted_iota(jnp.int32, (BAND, 2 * BAND), 1)
    in_prev = (kj < BAND) & (kj >= qi)
    in_cur = (kj >= BAND) & (kj - BAND <= qi)
    neg = jnp.float32(-jnp.inf)
    bias_in = jnp.where(in_prev | in_cur, 0.0, neg)
    bias_first = jnp.where(in_cur, 0.0, neg)
    bias_edge = jnp.where(c > 0, bias_in, bias_first)

    for g in tuple(range(1, N_GROUPS)) + (0,):
        dil = DILATIONS[g]
        q_ref, k_ref, v_ref = ins[3 * g:3 * g + 3]
        blocks = CHUNK // dil // BAND
        carry = [rg for r in range(dil) for rg in _band_rows(dil, r, blocks - 1)]
        for r in range(dil):
            for jb in range(blocks):
                cur = _band_rows(dil, r, jb)
                q = _gather(q_ref, cur)
                if jb == 0:
                    base = r * BAND
                    k_prev = kprev[g][base:base + BAND, :]
                    v_prev = vprev[g][base:base + BAND, :]
                    bias = bias_edge
                else:
                    prev = _band_rows(dil, r, jb - 1)
                    k_prev, v_prev = _gather(k_ref, prev), _gather(v_ref, prev)
                    bias = bias_in
                kcat = jnp.concatenate([k_prev, _gather(k_ref, cur)], axis=0)
                vcat = jnp.concatenate([v_prev, _gather(v_ref, cur)], axis=0)
                block = _attn_block(q, kcat, vcat, bias)
                if dil == 16:
                    rows = pl.ds((r % 4) * (CHUNK // 4) + r // 4, BAND, stride=4)
                    for ref, piece in zip(tmp, block):
                        ref[rows, :] = piece
                elif dil > 1:
                    rows = pl.ds(jb * BAND * dil + r, BAND, stride=dil)
                    for ref, piece in zip(nat[g], block):
                        ref[rows, :] = piece
                else:
                    a = jb * BAND
                    groups = [block] + [[ref[a:a + BAND, :] for ref in nat[gg]] for gg in range(1, N_GROUPS)]
                    m = functools.reduce(jnp.maximum, [mg for _, mg, _ in groups])
                    es = [jnp.exp2(mg - m) for _, mg, _ in groups]
                    num = sum(e * og for e, (og, _, _) in zip(es, groups))
                    den = sum(e * dg for e, (_, _, dg) in zip(es, groups))
                    out_ref[a:a + BAND, :] = (num / den).astype(BF16)
        if dil == 16:
            seg = CHUNK // 4
            for r4 in range(4):
                for dst, src in zip(nat[g], tmp):
                    dst[pl.ds(r4, seg, stride=4), :] = src[r4 * seg:(r4 + 1) * seg, :]
        off = 0
        for a, n in carry:
            kprev[g][off:off + n, :] = k_ref[a:a + n, :]
            vprev[g][off:off + n, :] = v_ref[a:a + n, :]
            off += n


def _attn(qkv):
    b, s, _ = qkv.shape
    aw = HEADS * HEAD_DIM
    n_in = 3 * N_GROUPS
    head_blk = lambda idx: pl.BlockSpec((None, CHUNK, HEAD_DIM), lambda bi, h, c: (bi, c, idx * HEADS + h))
    carry_rows = [BAND * dil for dil in DILATIONS]
    return pl.pallas_call(
        _attn_kernel,
        out_shape=jax.ShapeDtypeStruct((b, s, aw), BF16),
        grid=(b, HEADS, s // CHUNK),
        in_specs=[head_blk(idx) for idx in range(n_in)],
        out_specs=head_blk(0),
        scratch_shapes=(
            [pltpu.VMEM((n, HEAD_DIM), BF16) for n in carry_rows] * 2
            + [pltpu.VMEM((CHUNK, HEAD_DIM), F32)] * (3 * N_GROUPS)),
        compiler_params=_params(3),
        name="attn",
    )(*([qkv] * n_in))


def _mixout_kernel(x_ref, mod_ref, o_ref, yc_ref, sga_ref, sgc_ref, wa_ref, wc_ref, wo_ref, out_ref):
    for r0 in range(0, x_ref.shape[0], MIX_SUB_ROWS):
        rows = slice(r0, r0 + MIX_SUB_ROWS)
        y_attn = _dot(o_ref[rows, :], wa_ref[...])
        y_conv = _dot(yc_ref[rows, :], wc_ref[...])
        merged = (sga_ref[rows, :].astype(F32) * y_attn + sgc_ref[rows, :].astype(F32) * y_conv).astype(BF16)
        out_ref[rows, :] = x_ref[rows, :] + _mod_row(mod_ref, 5) * _dot(merged, wo_ref[...])


def _mixout(x, mod, o, mix, w_attn, w_conv, w_out, tm=1024):
    b, s, d = x.shape
    row = lambda width, col=0: pl.BlockSpec((None, tm, width), lambda bi, i: (bi, i, col))
    return pl.pallas_call(
        _mixout_kernel,
        out_shape=jax.ShapeDtypeStruct(x.shape, F32),
        grid=(b, s // tm),
        in_specs=[row(d), _mod_spec(d), row(o.shape[-1]),
                  row(d, 0), row(d, 1), row(d, 2),
                  _resident(w_attn.shape), _resident(w_conv.shape), _resident(w_out.shape)],
        out_specs=row(d),
        compiler_params=_params(2),
        name="mixout",
    )(x, mod, o, mix, mix, mix, w_attn, w_conv, w_out)


def kernel(x, c, w_ada, b_ada, norm_ffn1, ffn1_w_gate, ffn1_w_up, ffn1_w_down, norm_mix, w_in, q_norm, k_norm,
           conv_w, w_attn_branch, w_conv_branch, w_out, norm_ffn2, ffn2_w_gate, ffn2_w_up, ffn2_w_down):
    b, s, d = x.shape
    depth = w_ada.shape[0]
    for l in range(depth):
        mod, w_gate1, w_up1, w_down1 = _adaln(c, w_ada, b_ada, l,
                                              cast_ws=(ffn1_w_gate[l], ffn1_w_up[l], ffn1_w_down[l]))
        x, w_in_b = _ffn(x, mod, norm_ffn1[l][None, :], w_gate1, w_up1, w_down1, mod_row=0, cast_ws=(w_in[l],))
        qkv, mix, w_attn_b, w_conv_b, w_out_b, w_gate_b, w_up_b, w_down_b = _proj(
            x, mod, norm_mix[l][None, :], q_norm[l][None, :], k_norm[l][None, :], conv_w[l], w_in_b,
            cast_ws=(w_attn_branch[l], w_conv_branch[l], w_out[l], ffn2_w_gate[l], ffn2_w_up[l], ffn2_w_down[l]))
        o = _attn(qkv)
        x = _mixout(x, mod, o, mix, w_attn_b, w_conv_b, w_out_b)
        x, = _ffn(x, mod, norm_ffn2[l][None, :], w_gate_b, w_up_b, w_down_b, mod_row=6)
    return x
```

```python
import functools

import jax
import jax.numpy as jnp
from jax import lax
from jax.experimental import pallas as pl
from jax.experimental.pallas import tpu as pltpu

EPS = 1e-6
N_MOD = 9
N_GROUPS = 3
HEADS = 4
HEAD_DIM = 128
DILATIONS = (1, 4, 16)
BAND = 128
CHUNK = BAND * max(DILATIONS)
TILE = 512
FFN_SUB_ROWS = 256
MIX_SUB_ROWS = 512
PROJ_SUB_ROWS = 256
ATTN_HEADS_PER_STEP = 2
CONV_K = 3
LOG2_E = 1.4426950408889634

V7X_VMEM_BYTES = 64 * 1024 * 1024
VMEM_LIMIT = V7X_VMEM_BYTES - 8 * 1024 * 1024

BF16 = jnp.bfloat16
BF16_SUBLANES = 16
F32_SUBLANES = 8
F32 = jnp.float32


def _resident(shape):
    return pl.BlockSpec(shape, lambda *_: (0,) * len(shape), pipeline_mode=pl.Buffered(1))


def _params(n_axes):
    return pltpu.CompilerParams(dimension_semantics=("arbitrary",) * n_axes, vmem_limit_bytes=VMEM_LIMIT)


def _rms_mod(x, gain, shift, scale):
    ms = jnp.mean(x * x, axis=-1, keepdims=True)
    return x * lax.rsqrt(ms + EPS) * (gain * (1.0 + scale)) + shift


def _dot(a, b):
    return jnp.dot(a, b, preferred_element_type=F32)


def _cast_specs(ws, grid):
    n_steps = 1
    for g in grid:
        n_steps *= g
    specs, shapes = [], []
    for w in ws:
        rows, cols = w.shape
        n_bands = 1
        while n_bands * 2 <= n_steps:
            n_bands *= 2
        while rows % n_bands or (rows // n_bands) % BF16_SUBLANES:
            n_bands //= 2
        per = n_steps // n_bands

        def index(*ids, per=per, last=n_bands - 1):
            flat = ids[0]
            for i, g in zip(ids[1:], grid[1:]):
                flat = flat * g + i
            return (jnp.minimum(flat // per, last), 0)

        specs.append(pl.BlockSpec((rows // n_bands, cols), index))
        shapes.append(jax.ShapeDtypeStruct(w.shape, BF16))
    return specs, shapes


def _cast_bands(src_refs, dst_refs):
    for src, dst in zip(src_refs, dst_refs):
        dst[...] = src[...].astype(BF16)


def _adaln_kernel(n_cast, c_ref, w_ref, b_ref, *rest):
    o_ref = rest[n_cast]
    c = c_ref[...]
    c_act = c * jax.nn.sigmoid(c)
    lhs = jnp.concatenate([c_act, jnp.zeros((F32_SUBLANES - c.shape[0], c.shape[1]), F32)], axis=0)
    o_ref[...] = _dot(lhs, w_ref[...]) + b_ref[...]
    _cast_bands(rest[:n_cast], rest[n_cast + 1:])


def _adaln(c, w_ada, b_ada, layer, cast_ws=()):
    b, d = c.shape
    grid = (N_MOD,)
    cast_specs, cast_shapes = _cast_specs(cast_ws, grid)
    return pl.pallas_call(
        functools.partial(_adaln_kernel, len(cast_ws)),
        out_shape=[jax.ShapeDtypeStruct((N_MOD, F32_SUBLANES, d), F32)] + cast_shapes,
        grid=grid,
        in_specs=[
            pl.BlockSpec((b, d), lambda j: (0, 0)),
            pl.BlockSpec((None, d, d), lambda j: (layer, 0, j)),
            pl.BlockSpec((1, d), lambda j: (layer, j)),
        ] + cast_specs,
        out_specs=[pl.BlockSpec((None, F32_SUBLANES, d), lambda j: (j, 0, 0))] + cast_specs,
        compiler_params=_params(1),
        name="adaln",
    )(c, w_ada, b_ada, *cast_ws)


def _mod_row(mod_ref, k):
    return mod_ref[k, pl.ds(pl.program_id(0), 1), :]


def _mod_spec(d):
    return _resident((N_MOD, F32_SUBLANES, d))


def _ffn_kernel(mod_row, n_cast, x_ref, mod_ref, g_ref, wg_ref, wu_ref, wd_ref, *rest):
    o_ref = rest[n_cast]
    shift = _mod_row(mod_ref, mod_row)
    scale = _mod_row(mod_ref, mod_row + 1)
    gate = _mod_row(mod_ref, mod_row + 2)
    for r0 in range(0, x_ref.shape[0], FFN_SUB_ROWS):
        x = x_ref[r0:r0 + FFN_SUB_ROWS, :]
        h = _rms_mod(x, g_ref[...], shift, scale).astype(BF16)
        a = _dot(h, wg_ref[...])
        u = _dot(h, wu_ref[...])
        act = (a * jax.nn.sigmoid(a) * u).astype(BF16)
        y = _dot(act, wd_ref[...])
        o_ref[r0:r0 + FFN_SUB_ROWS, :] = x + 0.5 * gate * y
    _cast_bands(rest[:n_cast], rest[n_cast + 1:])


def _ffn(x, mod, gain, w_gate, w_up, w_down, mod_row, cast_ws=(), tm=1024):
    b, s, d = x.shape
    f = w_gate.shape[1]
    grid = (b, s // tm)
    cast_specs, cast_shapes = _cast_specs(cast_ws, grid)
    row = pl.BlockSpec((None, tm, d), lambda bi, i: (bi, i, 0))
    return pl.pallas_call(
        functools.partial(_ffn_kernel, mod_row, len(cast_ws)),
        out_shape=[jax.ShapeDtypeStruct(x.shape, F32)] + cast_shapes,
        grid=grid,
        in_specs=[
            row,
            _mod_spec(d),
            _resident((1, d)),
            _resident((d, f)),
            _resident((d, f)),
            _resident((f, d)),
        ] + cast_specs,
        out_specs=[row] + cast_specs,
        compiler_params=_params(2),
        name="ffn",
    )(x, mod, gain, w_gate, w_up, w_down, *cast_ws)


def _regroup_rows(src, dst_f32, dst_bf16, seg):
    q = seg // 4
    for base in range(0, TILE, seg):
        for r in range(4):
            lo = base + r * q
            for j in range(src.shape[0]):
                piece = src[j, pl.ds(base + r, q, stride=4), :]
                if dst_f32 is not None:
                    dst_f32[j, lo:lo + q, :] = piece
                dst_bf16[lo:lo + q, j * HEAD_DIM:(j + 1) * HEAD_DIM] = piece.astype(BF16)


def _proj_kernel(n_cast, x_ref, mod_ref, g_ref, qn_ref, kn_ref, cw_ref, w_ref, *rest):
    cast_in, rest = rest[:n_cast], rest[n_cast:]
    qkv_ref, mix_ref = rest[:2]
    cast_out = rest[2:2 + n_cast]
    scr_a, scr_b, h_scr, xc_scr = rest[2 + n_cast:]

    @pl.when(pl.program_id(1) == 0)
    def _():
        xc_scr[:F32_SUBLANES, :] = jnp.zeros((F32_SUBLANES, xc_scr.shape[1]), xc_scr.dtype)

    d = g_ref.shape[-1]
    aw = HEADS * HEAD_DIM
    subs = tuple(range(0, TILE, PROJ_SUB_ROWS))
    hs = []
    for r0 in subs:
        hf = _rms_mod(x_ref[r0:r0 + PROJ_SUB_ROWS, :], g_ref[...], _mod_row(mod_ref, 3), _mod_row(mod_ref, 4))
        hs.append(hf.astype(BF16))
        for j in range(d // HEAD_DIM):
            scr_a[j, r0:r0 + PROJ_SUB_ROWS, :] = hf[:, j * HEAD_DIM:(j + 1) * HEAD_DIM]
    _regroup_rows(scr_a, scr_b, h_scr.at[0], TILE)
    _regroup_rows(scr_b, None, h_scr.at[1], TILE // 4)
    o = 3 * N_GROUPS * aw
    pad = xc_scr.shape[0] - TILE
    for r0, h in zip(subs, hs):
        rows = slice(r0, r0 + PROJ_SUB_ROWS)
        mix_ref[rows, d:2 * d] = jax.nn.sigmoid(_dot(h, w_ref[:, o + 3 * d:o + 4 * d])).astype(BF16)
        mix_ref[rows, 2 * d:] = jax.nn.sigmoid(_dot(h, w_ref[:, o + 4 * d:o + 5 * d])).astype(BF16)
        xc = _dot(h, w_ref[:, o + 2 * d:o + 3 * d]) * _dot(h, w_ref[:, o:o + d])
        xc_scr[pad + r0:pad + r0 + PROJ_SUB_ROWS, :] = xc
        conv = xc * cw_ref[CONV_K - 1:CONV_K, :]
        for j in range(CONV_K - 1):
            back = CONV_K - 1 - j
            conv = conv + xc_scr[pad + r0 - back:pad + r0 - back + PROJ_SUB_ROWS, :] * cw_ref[j:j + 1, :]
        mix_ref[rows, :d] = (_dot(h, w_ref[:, o + d:o + 2 * d]) * conv).astype(BF16)
    xc_scr[:pad, :] = xc_scr[TILE:, :]
    gains = (qn_ref[...] * (HEAD_DIM ** -0.5 * LOG2_E), kn_ref[...], None)
    for j in range(3):
        for g in range(N_GROUPS):
            c0 = (j * N_GROUPS + g) * aw
            for r0, h in zip(subs, hs):
                lhs = h if g == 0 else h_scr[g - 1, r0:r0 + PROJ_SUB_ROWS, :]
                t = _dot(lhs, w_ref[:, c0:c0 + aw])
                for hd in range(HEADS):
                    col = hd * HEAD_DIM
                    slab = t[:, col:col + HEAD_DIM]
                    if gains[j] is not None:
                        ms = jnp.mean(slab * slab, axis=-1, keepdims=True)
                        slab = slab * lax.rsqrt(ms + EPS) * gains[j]
                    dst = (3 * g + j) * aw + col
                    qkv_ref[r0:r0 + PROJ_SUB_ROWS, dst:dst + HEAD_DIM] = slab.astype(BF16)
    _cast_bands(cast_in, cast_out)


def _proj(x, mod, gain, q_norm, k_norm, conv_w, w, cast_ws=()):
    b, s, d = x.shape
    aw = HEADS * HEAD_DIM
    grid = (b, s // TILE)
    cast_specs, cast_shapes = _cast_specs(cast_ws, grid)
    row = lambda width: pl.BlockSpec((None, TILE, width), lambda bi, i: (bi, i, 0))
    qkv_w, mix_w = 3 * N_GROUPS * aw, 3 * d
    return pl.pallas_call(
        functools.partial(_proj_kernel, len(cast_ws)),
        out_shape=[jax.ShapeDtypeStruct((b, s, qkv_w), BF16), jax.ShapeDtypeStruct((b, s, mix_w), BF16)] + cast_shapes,
        grid=grid,
        in_specs=[
            row(d),
            _mod_spec(d),
            _resident((1, d)),
            _resident((1, HEAD_DIM)),
            _resident((1, HEAD_DIM)),
            _resident(conv_w.shape),
            _resident(w.shape),
        ] + cast_specs,
        out_specs=[row(qkv_w), row(mix_w)] + cast_specs,
        scratch_shapes=[pltpu.VMEM((d // HEAD_DIM, TILE, HEAD_DIM), F32), pltpu.VMEM((d // HEAD_DIM, TILE, HEAD_DIM), F32),
                        pltpu.VMEM((2, TILE, d), BF16), pltpu.VMEM((F32_SUBLANES + TILE, d), F32)],
        compiler_params=_params(2),
        name="proj",
    )(x, mod, gain, q_norm, k_norm, conv_w, w, *cast_ws)


def _attn_block(q, kcat, vcat, bias):
    s = lax.dot_general(q, kcat, (((1,), (1,)), ((), ())), preferred_element_type=F32) + bias
    m = jnp.max(s, axis=-1, keepdims=True)
    p = jnp.exp2(s - m)
    od = _dot(p.astype(BF16), jnp.concatenate([vcat, jnp.ones_like(vcat)], axis=1))
    return od[:, :HEAD_DIM], jnp.broadcast_to(m, (BAND, HEAD_DIM)), od[:, HEAD_DIM:]


def _band_rows(dil, r, jb):
    per_tile = TILE // dil
    if dil == 1:
        return [(jb * BAND, BAND)]
    pos = r if dil == 4 else (r % 4) * 4 + r // 4
    tiles = BAND // per_tile
    return [((jb * tiles + t) * TILE + pos * per_tile, per_tile) for t in range(tiles)]


def _gather(ref, ranges, lanes):
    parts = [ref[a:a + n, lanes] for a, n in ranges]
    return parts[0] if len(parts) == 1 else jnp.concatenate(parts, axis=0)


def _attn_kernel(*refs):
    ins = refs[:3 * N_GROUPS]
    out_ref = refs[3 * N_GROUPS]
    scr = refs[3 * N_GROUPS + 1:]
    per_head = len(scr) // ATTN_HEADS_PER_STEP
    c = pl.program_id(2)

    @pl.when(c == 0)
    def _():
        for hh in range(ATTN_HEADS_PER_STEP):
            for ref in scr[hh * per_head:hh * per_head + 2 * N_GROUPS]:
                ref[...] = jnp.zeros(ref.shape, ref.dtype)

    qi = lax.broadcasted_iota(jnp.int32, (BAND, 2 * BAND), 0)
    kj = lax.broadcasted_iota(jnp.int32, (BAND, 2 * BAND), 1)
    in_prev = (kj < BAND) & (kj >= qi)
    in_cur = (kj >= BAND) & (kj - BAND <= qi)
    neg = jnp.float32(-jnp.inf)
    bias_in = jnp.where(in_prev | in_cur, 0.0, neg)
    bias_first = jnp.where(in_cur, 0.0, neg)
    bias_edge = jnp.where(c > 0, bias_in, bias_first)
    for hh in range(ATTN_HEADS_PER_STEP):
        _attn_head(ins, out_ref, slice(hh * HEAD_DIM, (hh + 1) * HEAD_DIM), scr[hh * per_head:(hh + 1) * per_head],
                   bias_in, bias_edge)


def _attn_head(ins, out_ref, lanes, scr, bias_in, bias_edge):
    kprev, vprev = scr[0:N_GROUPS], scr[N_GROUPS:2 * N_GROUPS]
    pieces = scr[2 * N_GROUPS:]
    nat = (None,) + tuple(pieces[3 * (g - 1):3 * g] for g in range(1, N_GROUPS))
    tmp = pieces[3 * (N_GROUPS - 1):]
    for g in tuple(range(1, N_GROUPS)) + (0,):
        dil = DILATIONS[g]
        q_ref, k_ref, v_ref = ins[3 * g:3 * g + 3]
        blocks = CHUNK // dil // BAND
        carry = [rg for r in range(dil) for rg in _band_rows(dil, r, blocks - 1)]
        for r in range(dil):
            for jb in range(blocks):
                cur = _band_rows(dil, r, jb)
                q = _gather(q_ref, cur, lanes)
                if jb == 0:
                    base = r * BAND
                    k_prev = kprev[g][base:base + BAND, :]
                    v_prev = vprev[g][base:base + BAND, :]
                    bias = bias_edge
                else:
                    prev = _band_rows(dil, r, jb - 1)
                    k_prev, v_prev = _gather(k_ref, prev, lanes), _gather(v_ref, prev, lanes)
                    bias = bias_in
                kcat = jnp.concatenate([k_prev, _gather(k_ref, cur, lanes)], axis=0)
                vcat = jnp.concatenate([v_prev, _gather(v_ref, cur, lanes)], axis=0)
                block = _attn_block(q, kcat, vcat, bias)
                if dil == 16:
                    rows = pl.ds((r % 4) * (CHUNK // 4) + r // 4, BAND, stride=4)
                    for ref, piece in zip(tmp, block):
                        ref[rows, :] = piece
                elif dil > 1:
                    rows = pl.ds(jb * BAND * dil + r, BAND, stride=dil)
                    for ref, piece in zip(nat[g], block):
                        ref[rows, :] = piece
                else:
                    a = jb * BAND
                    groups = [block] + [[ref[a:a + BAND, :] for ref in nat[gg]] for gg in range(1, N_GROUPS)]
                    m = functools.reduce(jnp.maximum, [mg for _, mg, _ in groups])
                    es = [jnp.exp2(mg - m) for _, mg, _ in groups]
                    num = sum(e * og for e, (og, _, _) in zip(es, groups))
                    den = sum(e * dg for e, (_, _, dg) in zip(es, groups))
                    out_ref[a:a + BAND, lanes] = (num / den).astype(BF16)
        if dil == 16:
            seg = CHUNK // 4
            for r4 in range(4):
                for dst, src in zip(nat[g], tmp):
                    dst[pl.ds(r4, seg, stride=4), :] = src[r4 * seg:(r4 + 1) * seg, :]
        off = 0
        for a, n in carry:
            kprev[g][off:off + n, :] = k_ref[a:a + n, lanes]
            vprev[g][off:off + n, :] = v_ref[a:a + n, lanes]
            off += n


def _attn(qkv):
    b, s, _ = qkv.shape
    aw = HEADS * HEAD_DIM
    n_in = 3 * N_GROUPS
    steps_h = HEADS // ATTN_HEADS_PER_STEP
    head_blk = lambda idx: pl.BlockSpec((None, CHUNK, ATTN_HEADS_PER_STEP * HEAD_DIM),
                                        lambda bi, h, c: (bi, c, idx * steps_h + h))
    carry_rows = [BAND * dil for dil in DILATIONS]
    head_scratch = ([pltpu.VMEM((n, HEAD_DIM), BF16) for n in carry_rows] * 2
                    + [pltpu.VMEM((CHUNK, HEAD_DIM), F32)] * (3 * N_GROUPS))
    return pl.pallas_call(
        _attn_kernel,
        out_shape=jax.ShapeDtypeStruct((b, s, aw), BF16),
        grid=(b, steps_h, s // CHUNK),
        in_specs=[head_blk(idx) for idx in range(n_in)],
        out_specs=head_blk(0),
        scratch_shapes=head_scratch * ATTN_HEADS_PER_STEP,
        compiler_params=_params(3),
        name="attn",
    )(*([qkv] * n_in))


def _mixout_kernel(x_ref, mod_ref, o_ref, yc_ref, sga_ref, sgc_ref, wa_ref, wc_ref, wo_ref, out_ref):
    for r0 in range(0, x_ref.shape[0], MIX_SUB_ROWS):
        rows = slice(r0, r0 + MIX_SUB_ROWS)
        y_attn = _dot(o_ref[rows, :], wa_ref[...])
        y_conv = _dot(yc_ref[rows, :], wc_ref[...])
        merged = (sga_ref[rows, :].astype(F32) * y_attn + sgc_ref[rows, :].astype(F32) * y_conv).astype(BF16)
        out_ref[rows, :] = x_ref[rows, :] + _mod_row(mod_ref, 5) * _dot(merged, wo_ref[...])


def _mixout(x, mod, o, mix, w_attn, w_conv, w_out, tm=1024):
    b, s, d = x.shape
    row = lambda width, col=0: pl.BlockSpec((None, tm, width), lambda bi, i: (bi, i, col))
    return pl.pallas_call(
        _mixout_kernel,
        out_shape=jax.ShapeDtypeStruct(x.shape, F32),
        grid=(b, s // tm),
        in_specs=[row(d), _mod_spec(d), row(o.shape[-1]),
                  row(d, 0), row(d, 1), row(d, 2),
                  _resident(w_attn.shape), _resident(w_conv.shape), _resident(w_out.shape)],
        out_specs=row(d),
        compiler_params=_params(2),
        name="mixout",
    )(x, mod, o, mix, mix, mix, w_attn, w_conv, w_out)


def kernel(x, c, w_ada, b_ada, norm_ffn1, ffn1_w_gate, ffn1_w_up, ffn1_w_down, norm_mix, w_in, q_norm, k_norm,
           conv_w, w_attn_branch, w_conv_branch, w_out, norm_ffn2, ffn2_w_gate, ffn2_w_up, ffn2_w_down):
    b, s, d = x.shape
    depth = w_ada.shape[0]
    for l in range(depth):
        mod, w_gate1, w_up1, w_down1 = _adaln(c, w_ada, b_ada, l,
                                              cast_ws=(ffn1_w_gate[l], ffn1_w_up[l], ffn1_w_down[l]))
        x, w_in_b = _ffn(x, mod, norm_ffn1[l][None, :], w_gate1, w_up1, w_down1, mod_row=0, cast_ws=(w_in[l],))
        qkv, mix, w_attn_b, w_conv_b, w_out_b, w_gate_b, w_up_b, w_down_b = _proj(
            x, mod, norm_mix[l][None, :], q_norm[l][None, :], k_norm[l][None, :], conv_w[l], w_in_b,
            cast_ws=(w_attn_branch[l], w_conv_branch[l], w_out[l], ffn2_w_gate[l], ffn2_w_up[l], ffn2_w_down[l]))
        o = _attn(qkv)
        x = _mixout(x, mod, o, mix, w_attn_b, w_conv_b, w_out_b)
        x, = _ffn(x, mod, norm_ffn2[l][None, :], w_gate_b, w_up_b, w_down_b, mod_row=6)
    return x
```

```python
import functools

import jax
import jax.numpy as jnp
from jax import lax
from jax.experimental import pallas as pl
from jax.experimental.pallas import tpu as pltpu

EPS = 1e-6
N_MOD = 9
N_GROUPS = 3
HEADS = 4
HEAD_DIM = 128
DILATIONS = (1, 4, 16)
BAND = 128
CHUNK = BAND * max(DILATIONS)
TILE = 512
FFN_SUB_ROWS = 256
MIX_SUB_ROWS = 512
PROJ_SUB_ROWS = 256
ATTN_HEADS_PER_STEP = 2
CONV_K = 3
LOG2_E = 1.4426950408889634

V7X_VMEM_BYTES = 64 * 1024 * 1024
VMEM_LIMIT = V7X_VMEM_BYTES - 8 * 1024 * 1024

BF16 = jnp.bfloat16
BF16_SUBLANES = 16
F32_SUBLANES = 8
F32 = jnp.float32


def _resident(shape):
    return pl.BlockSpec(shape, lambda *_: (0,) * len(shape), pipeline_mode=pl.Buffered(1))


def _params(n_axes):
    return pltpu.CompilerParams(dimension_semantics=("arbitrary",) * n_axes, vmem_limit_bytes=VMEM_LIMIT)


def _rms_mod(x, gain, shift, scale):
    ms = jnp.mean(x * x, axis=-1, keepdims=True)
    return x * lax.rsqrt(ms + EPS) * (gain * (1.0 + scale)) + shift


def _dot(a, b):
    return jnp.dot(a, b, preferred_element_type=F32)


def _cast_specs(ws, grid):
    n_steps = 1
    for g in grid:
        n_steps *= g
    specs, shapes = [], []
    for w in ws:
        rows, cols = w.shape
        n_bands = 1
        while n_bands * 2 <= n_steps:
            n_bands *= 2
        while rows % n_bands or (rows // n_bands) % BF16_SUBLANES:
            n_bands //= 2
        per = n_steps // n_bands

        def index(*ids, per=per, last=n_bands - 1):
            flat = ids[0]
            for i, g in zip(ids[1:], grid[1:]):
                flat = flat * g + i
            return (jnp.minimum(flat // per, last), 0)

        specs.append(pl.BlockSpec((rows // n_bands, cols), index))
        shapes.append(jax.ShapeDtypeStruct(w.shape, BF16))
    return specs, shapes


def _cast_bands(src_refs, dst_refs):
    for src, dst in zip(src_refs, dst_refs):
        dst[...] = src[...].astype(BF16)


def _adaln_kernel(n_cast, c_ref, w_ref, b_ref, *rest):
    o_ref = rest[n_cast]
    c = c_ref[...]
    c_act = c * jax.nn.sigmoid(c)
    lhs = jnp.concatenate([c_act, jnp.zeros((F32_SUBLANES - c.shape[0], c.shape[1]), F32)], axis=0)
    o_ref[...] = _dot(lhs, w_ref[...]) + b_ref[...]
    _cast_bands(rest[:n_cast], rest[n_cast + 1:])


def _adaln(c, w_ada, b_ada, layer, cast_ws=()):
    b, d = c.shape
    grid = (N_MOD,)
    cast_specs, cast_shapes = _cast_specs(cast_ws, grid)
    return pl.pallas_call(
        functools.partial(_adaln_kernel, len(cast_ws)),
        out_shape=[jax.ShapeDtypeStruct((N_MOD, F32_SUBLANES, d), F32)] + cast_shapes,
        grid=grid,
        in_specs=[
            pl.BlockSpec((b, d), lambda j: (0, 0)),
            pl.BlockSpec((None, d, d), lambda j: (layer, 0, j)),
            pl.BlockSpec((1, d), lambda j: (layer, j)),
        ] + cast_specs,
        out_specs=[pl.BlockSpec((None, F32_SUBLANES, d), lambda j: (j, 0, 0))] + cast_specs,
        compiler_params=_params(1),
        name="adaln",
    )(c, w_ada, b_ada, *cast_ws)


def _mod_row(mod_ref, k):
    return mod_ref[k, pl.ds(pl.program_id(0), 1), :]


def _mod_spec(d):
    return _resident((N_MOD, F32_SUBLANES, d))


def _ffn_kernel(mod_row, n_cast, x_ref, mod_ref, g_ref, wg_ref, wu_ref, wd_ref, *rest):
    o_ref = rest[n_cast]
    shift = _mod_row(mod_ref, mod_row)
    scale = _mod_row(mod_ref, mod_row + 1)
    gate = _mod_row(mod_ref, mod_row + 2)
    for r0 in range(0, x_ref.shape[0], FFN_SUB_ROWS):
        x = x_ref[r0:r0 + FFN_SUB_ROWS, :]
        h = _rms_mod(x, g_ref[...], shift, scale).astype(BF16)
        a = _dot(h, wg_ref[...])
        u = _dot(h, wu_ref[...])
        act = (a * jax.nn.sigmoid(a) * u).astype(BF16)
        y = _dot(act, wd_ref[...])
        o_ref[r0:r0 + FFN_SUB_ROWS, :] = x + 0.5 * gate * y
    _cast_bands(rest[:n_cast], rest[n_cast + 1:])


def _ffn(x, mod, gain, w_gate, w_up, w_down, mod_row, cast_ws=(), tm=1024):
    b, s, d = x.shape
    f = w_gate.shape[1]
    grid = (b, s // tm)
    cast_specs, cast_shapes = _cast_specs(cast_ws, grid)
    row = pl.BlockSpec((None, tm, d), lambda bi, i: (bi, i, 0))
    return pl.pallas_call(
        functools.partial(_ffn_kernel, mod_row, len(cast_ws)),
        out_shape=[jax.ShapeDtypeStruct(x.shape, F32)] + cast_shapes,
        grid=grid,
        in_specs=[
            row,
            _mod_spec(d),
            _resident((1, d)),
            _resident((d, f)),
            _resident((d, f)),
            _resident((f, d)),
        ] + cast_specs,
        out_specs=[row] + cast_specs,
        compiler_params=_params(2),
        name="ffn",
    )(x, mod, gain, w_gate, w_up, w_down, *cast_ws)


def _regroup_rows(src, dst_f32, dst_bf16, seg):
    q = seg // 4
    for base in range(0, TILE, seg):
        for r in range(4):
            lo = base + r * q
            for j in range(src.shape[0]):
                piece = src[j, pl.ds(base + r, q, stride=4), :]
                if dst_f32 is not None:
                    dst_f32[j, lo:lo + q, :] = piece
                dst_bf16[lo:lo + q, j * HEAD_DIM:(j + 1) * HEAD_DIM] = piece.astype(BF16)


def _proj_kernel(n_cast, x_ref, mod_ref, g_ref, qn_ref, kn_ref, cw_ref, w_ref, *rest):
    cast_in, rest = rest[:n_cast], rest[n_cast:]
    qkv_ref, mix_ref = rest[:2]
    cast_out = rest[2:2 + n_cast]
    scr_a, scr_b, h_scr, xc_scr = rest[2 + n_cast:]

    @pl.when(pl.program_id(1) == 0)
    def _():
        xc_scr[:F32_SUBLANES, :] = jnp.zeros((F32_SUBLANES, xc_scr.shape[1]), xc_scr.dtype)

    d = g_ref.shape[-1]
    aw = HEADS * HEAD_DIM
    subs = tuple(range(0, TILE, PROJ_SUB_ROWS))
    hs = []
    for r0 in subs:
        hf = _rms_mod(x_ref[r0:r0 + PROJ_SUB_ROWS, :], g_ref[...], _mod_row(mod_ref, 3), _mod_row(mod_ref, 4))
        hs.append(hf.astype(BF16))
        for j in range(d // HEAD_DIM):
            scr_a[j, r0:r0 + PROJ_SUB_ROWS, :] = hf[:, j * HEAD_DIM:(j + 1) * HEAD_DIM]
    _regroup_rows(scr_a, scr_b, h_scr.at[0], TILE)
    _regroup_rows(scr_b, None, h_scr.at[1], TILE // 4)
    o = 3 * N_GROUPS * aw
    pad = xc_scr.shape[0] - TILE
    for r0, h in zip(subs, hs):
        rows = slice(r0, r0 + PROJ_SUB_ROWS)
        mix_ref[rows, d:2 * d] = jax.nn.sigmoid(_dot(h, w_ref[:, o + 3 * d:o + 4 * d])).astype(BF16)
        mix_ref[rows, 2 * d:] = jax.nn.sigmoid(_dot(h, w_ref[:, o + 4 * d:o + 5 * d])).astype(BF16)
        xc = _dot(h, w_ref[:, o + 2 * d:o + 3 * d]) * _dot(h, w_ref[:, o:o + d])
        xc_scr[pad + r0:pad + r0 + PROJ_SUB_ROWS, :] = xc
        conv = xc * cw_ref[CONV_K - 1:CONV_K, :]
        for j in range(CONV_K - 1):
            back = CONV_K - 1 - j
            conv = conv + xc_scr[pad + r0 - back:pad + r0 - back + PROJ_SUB_ROWS, :] * cw_ref[j:j + 1, :]
        mix_ref[rows, :d] = (_dot(h, w_ref[:, o + d:o + 2 * d]) * conv).astype(BF16)
    xc_scr[:pad, :] = xc_scr[TILE:, :]
    gains = (qn_ref[...] * (HEAD_DIM ** -0.5 * LOG2_E), kn_ref[...], None)
    for j in range(3):
        for g in range(N_GROUPS):
            c0 = (j * N_GROUPS + g) * aw
            for r0, h in zip(subs, hs):
                lhs = h if g == 0 else h_scr[g - 1, r0:r0 + PROJ_SUB_ROWS, :]
                t = _dot(lhs, w_ref[:, c0:c0 + aw])
                for hd in range(HEADS):
                    col = hd * HEAD_DIM
                    slab = t[:, col:col + HEAD_DIM]
                    if gains[j] is not None:
                        ms = jnp.mean(slab * slab, axis=-1, keepdims=True)
                        slab = slab * lax.rsqrt(ms + EPS) * gains[j]
                    dst = (3 * g + j) * aw + col
                    qkv_ref[r0:r0 + PROJ_SUB_ROWS, dst:dst + HEAD_DIM] = slab.astype(BF16)
    _cast_bands(cast_in, cast_out)


def _proj(x, mod, gain, q_norm, k_norm, conv_w, w, cast_ws=()):
    b, s, d = x.shape
    aw = HEADS * HEAD_DIM
    grid = (b, s // TILE)
    cast_specs, cast_shapes = _cast_specs(cast_ws, grid)
    row = lambda width: pl.BlockSpec((None, TILE, width), lambda bi, i: (bi, i, 0))
    qkv_w, mix_w = 3 * N_GROUPS * aw, 3 * d
    return pl.pallas_call(
        functools.partial(_proj_kernel, len(cast_ws)),
        out_shape=[jax.ShapeDtypeStruct((b, s, qkv_w), BF16), jax.ShapeDtypeStruct((b, s, mix_w), BF16)] + cast_shapes,
        grid=grid,
        in_specs=[
            row(d),
            _mod_spec(d),
            _resident((1, d)),
            _resident((1, HEAD_DIM)),
            _resident((1, HEAD_DIM)),
            _resident(conv_w.shape),
            _resident(w.shape),
        ] + cast_specs,
        out_specs=[row(qkv_w), row(mix_w)] + cast_specs,
        scratch_shapes=[pltpu.VMEM((d // HEAD_DIM, TILE, HEAD_DIM), F32), pltpu.VMEM((d // HEAD_DIM, TILE, HEAD_DIM), F32),
                        pltpu.VMEM((2, TILE, d), BF16), pltpu.VMEM((F32_SUBLANES + TILE, d), F32)],
        compiler_params=_params(2),
        name="proj",
    )(x, mod, gain, q_norm, k_norm, conv_w, w, *cast_ws)


def _attn_block(q, kcat, vcat, bias):
    s = lax.dot_general(q, kcat, (((1,), (1,)), ((), ())), preferred_element_type=F32) + bias
    m = jnp.max(s, axis=-1, keepdims=True)
    p = jnp.exp2(s - m)
    od = _dot(p.astype(BF16), jnp.concatenate([vcat, jnp.ones_like(vcat)], axis=1))
    return od[:, :HEAD_DIM], jnp.broadcast_to(m, (BAND, HEAD_DIM)), od[:, HEAD_DIM:]


def _band_rows(dil, r, jb):
    per_tile = TILE // dil
    if dil == 1:
        return [(jb * BAND, BAND)]
    pos = r if dil == 4 else (r % 4) * 4 + r // 4
    tiles = BAND // per_tile
    return [((jb * tiles + t) * TILE + pos * per_tile, per_tile) for t in range(tiles)]


def _gather(ref, ranges, lanes):
    parts = [ref[a:a + n, lanes] for a, n in ranges]
    return parts[0] if len(parts) == 1 else jnp.concatenate(parts, axis=0)


def _attn_kernel(*refs):
    ins = refs[:3 * N_GROUPS]
    out_ref = refs[3 * N_GROUPS]
    scr = refs[3 * N_GROUPS + 1:]
    per_head = len(scr) // ATTN_HEADS_PER_STEP
    c = pl.program_id(2)

    @pl.when(c == 0)
    def _():
        for hh in range(ATTN_HEADS_PER_STEP):
            for ref in scr[hh * per_head:hh * per_head + 2 * N_GROUPS]:
                ref[...] = jnp.zeros(ref.shape, ref.dtype)

    qi = lax.broadcasted_iota(jnp.int32, (BAND, 2 * BAND), 0)
    kj = lax.broadcasted_iota(jnp.int32, (BAND, 2 * BAND), 1)
    in_prev = (kj < BAND) & (kj >= qi)
    in_cur = (kj >= BAND) & (kj - BAND <= qi)
    neg = jnp.float32(-jnp.inf)
    bias_in = jnp.where(in_prev | in_cur, 0.0, neg)
    bias_first = jnp.where(in_cur, 0.0, neg)
    bias_edge = jnp.where(c > 0, bias_in, bias_first)
    for hh in range(ATTN_HEADS_PER_STEP):
        _attn_head(ins, out_ref, slice(hh * HEAD_DIM, (hh + 1) * HEAD_DIM), scr[hh * per_head:(hh + 1) * per_head],
                   bias_in, bias_edge)


def _attn_head(ins, out_ref, lanes, scr, bias_in, bias_edge):
    kprev, vprev = scr[0:N_GROUPS], scr[N_GROUPS:2 * N_GROUPS]
    pieces = scr[2 * N_GROUPS:]
    nat = (None,) + tuple(pieces[3 * (g - 1):3 * g] for g in range(1, N_GROUPS))
    tmp = pieces[3 * (N_GROUPS - 1):]
    for g in tuple(range(1, N_GROUPS)) + (0,):
        dil = DILATIONS[g]
        q_ref, k_ref, v_ref = ins[3 * g:3 * g + 3]
        blocks = CHUNK // dil // BAND
        carry = [rg for r in range(dil) for rg in _band_rows(dil, r, blocks - 1)]
        for r in range(dil):
            for jb in range(blocks):
                cur = _band_rows(dil, r, jb)
                q = _gather(q_ref, cur, lanes)
                if jb == 0:
                    base = r * BAND
                    k_prev = kprev[g][base:base + BAND, :]
                    v_prev = vprev[g][base:base + BAND, :]
                    bias = bias_edge
                else:
                    prev = _band_rows(dil, r, jb - 1)
                    k_prev, v_prev = _gather(k_ref, prev, lanes), _gather(v_ref, prev, lanes)
                    bias = bias_in
                kcat = jnp.concatenate([k_prev, _gather(k_ref, cur, lanes)], axis=0)
                vcat = jnp.concatenate([v_prev, _gather(v_ref, cur, lanes)], axis=0)
                block = _attn_block(q, kcat, vcat, bias)
                if dil == 16:
                    rows = pl.ds((r % 4) * (CHUNK // 4) + r // 4, BAND, stride=4)
                    for ref, piece in zip(tmp, block):
                        ref[rows, :] = piece
                elif dil > 1:
                    rows = pl.ds(jb * BAND * dil + r, BAND, stride=dil)
                    for ref, piece in zip(nat[g], block):
                        ref[rows, :] = piece
                else:
                    a = jb * BAND
                    groups = [block] + [[ref[a:a + BAND, :] for ref in nat[gg]] for gg in range(1, N_GROUPS)]
                    m = functools.reduce(jnp.maximum, [mg for _, mg, _ in groups])
                    es = [jnp.exp2(mg - m) for _, mg, _ in groups]
                    num = sum(e * og for e, (og, _, _) in zip(es, groups))
                    den = sum(e * dg for e, (_, _, dg) in zip(es, groups))
                    out_ref[a:a + BAND, lanes] = (num / den).astype(BF16)
        if dil == 16:
            seg = CHUNK // 4
            for r4 in range(4):
                for dst, src in zip(nat[g], tmp):
                    dst[pl.ds(r4, seg, stride=4), :] = src[r4 * seg:(r4 + 1) * seg, :]
        off = 0
        for a, n in carry:
            kprev[g][off:off + n, :] = k_ref[a:a + n, lanes]
            vprev[g][off:off + n, :] = v_ref[a:a + n, lanes]
            off += n


def _attn(qkv):
    b, s, _ = qkv.shape
    aw = HEADS * HEAD_DIM
    n_in = 3 * N_GROUPS
    steps_h = HEADS // ATTN_HEADS_PER_STEP
    head_blk = lambda idx: pl.BlockSpec((None, CHUNK, ATTN_HEADS_PER_STEP * HEAD_DIM),
                                        lambda bi, h, c: (bi, c, idx * steps_h + h))
    carry_rows = [BAND * dil for dil in DILATIONS]
    head_scratch = ([pltpu.VMEM((n, HEAD_DIM), BF16) for n in carry_rows] * 2
                    + [pltpu.VMEM((CHUNK, HEAD_DIM), F32)] * (3 * N_GROUPS))
    return pl.pallas_call(
        _attn_kernel,
        out_shape=jax.ShapeDtypeStruct((b, s, aw), BF16),
        grid=(b, steps_h, s // CHUNK),
        in_specs=[head_blk(idx) for idx in range(n_in)],
        out_specs=head_blk(0),
        scratch_shapes=head_scratch * ATTN_HEADS_PER_STEP,
        compiler_params=_params(3),
        name="attn",
    )(*([qkv] * n_in))


def _mixout_kernel(x_ref, mod_ref, o_ref, mix_ref, wa_ref, wc_ref, wo_ref, out_ref):
    d = x_ref.shape[-1]
    for r0 in range(0, x_ref.shape[0], MIX_SUB_ROWS):
        rows = slice(r0, r0 + MIX_SUB_ROWS)
        y_attn = _dot(o_ref[rows, :], wa_ref[...])
        y_conv = _dot(mix_ref[rows, :d], wc_ref[...])
        merged = (mix_ref[rows, d:2 * d].astype(F32) * y_attn + mix_ref[rows, 2 * d:].astype(F32) * y_conv).astype(BF16)
        out_ref[rows, :] = x_ref[rows, :] + _mod_row(mod_ref, 5) * _dot(merged, wo_ref[...])


def _mixout(x, mod, o, mix, w_attn, w_conv, w_out, tm=1024):
    b, s, d = x.shape
    row = lambda width: pl.BlockSpec((None, tm, width), lambda bi, i: (bi, i, 0))
    return pl.pallas_call(
        _mixout_kernel,
        out_shape=jax.ShapeDtypeStruct(x.shape, F32),
        grid=(b, s // tm),
        in_specs=[row(d), _mod_spec(d), row(o.shape[-1]), row(mix.shape[-1]),
                  _resident(w_attn.shape), _resident(w_conv.shape), _resident(w_out.shape)],
        out_specs=row(d),
        compiler_params=_params(2),
        name="mixout",
    )(x, mod, o, mix, w_attn, w_conv, w_out)


def kernel(x, c, w_ada, b_ada, norm_ffn1, ffn1_w_gate, ffn1_w_up, ffn1_w_down, norm_mix, w_in, q_norm, k_norm,
           conv_w, w_attn_branch, w_conv_branch, w_out, norm_ffn2, ffn2_w_gate, ffn2_w_up, ffn2_w_down):
    b, s, d = x.shape
    depth = w_ada.shape[0]
    for l in range(depth):
        mod, w_gate1, w_up1, w_down1 = _adaln(c, w_ada, b_ada, l,
                                              cast_ws=(ffn1_w_gate[l], ffn1_w_up[l], ffn1_w_down[l]))
        x, w_in_b = _ffn(x, mod, norm_ffn1[l][None, :], w_gate1, w_up1, w_down1, mod_row=0, cast_ws=(w_in[l],))
        qkv, mix, w_attn_b, w_conv_b, w_out_b, w_gate_b, w_up_b, w_down_b = _proj(
            x, mod, norm_mix[l][None, :], q_norm[l][None, :], k_norm[l][None, :], conv_w[l], w_in_b,
            cast_ws=(w_attn_branch[l], w_conv_branch[l], w_out[l], ffn2_w_gate[l], ffn2_w_up[l], ffn2_w_down[l]))
        o = _attn(qkv)
        x = _mixout(x, mod, o, mix, w_attn_b, w_conv_b, w_out_b)
        x, = _ffn(x, mod, norm_ffn2[l][None, :], w_gate_b, w_up_b, w_down_b, mod_row=6)
    return x
```

```python
import functools

import jax
import jax.numpy as jnp
from jax import lax
from jax.experimental import pallas as pl
from jax.experimental.pallas import tpu as pltpu

EPS = 1e-6
N_MOD = 9
N_GROUPS = 3
HEADS = 4
HEAD_DIM = 128
DILATIONS = (1, 4, 16)
BAND = 128
CHUNK = BAND * max(DILATIONS)
TILE = 512
FFN_SUB_ROWS = 256
MIX_SUB_ROWS = 512
PROJ_SUB_ROWS = 256
ATTN_HEADS_PER_STEP = 2
CONV_K = 3
LOG2_E = 1.4426950408889634

V7X_VMEM_BYTES = 64 * 1024 * 1024
VMEM_LIMIT = V7X_VMEM_BYTES - 8 * 1024 * 1024

BF16 = jnp.bfloat16
BF16_SUBLANES = 16
F32_SUBLANES = 8
F32 = jnp.float32


def _resident(shape):
    return pl.BlockSpec(shape, lambda *_: (0,) * len(shape), pipeline_mode=pl.Buffered(1))


def _params(n_axes):
    return pltpu.CompilerParams(dimension_semantics=("arbitrary",) * n_axes, vmem_limit_bytes=VMEM_LIMIT)


def _rms_mod(x, gain, shift, scale):
    ms = jnp.mean(x * x, axis=-1, keepdims=True)
    return x * lax.rsqrt(ms + EPS) * (gain * (1.0 + scale)) + shift


def _dot(a, b):
    return jnp.dot(a, b, preferred_element_type=F32)


def _cast_specs(ws, grid):
    n_steps = 1
    for g in grid:
        n_steps *= g
    specs, shapes = [], []
    for w in ws:
        rows, cols = w.shape
        n_bands = 1
        while n_bands * 2 <= n_steps:
            n_bands *= 2
        while rows % n_bands or (rows // n_bands) % BF16_SUBLANES:
            n_bands //= 2
        per = n_steps // n_bands

        def index(*ids, per=per, last=n_bands - 1):
            flat = ids[0]
            for i, g in zip(ids[1:], grid[1:]):
                flat = flat * g + i
            return (jnp.minimum(flat // per, last), 0)

        specs.append(pl.BlockSpec((rows // n_bands, cols), index))
        shapes.append(jax.ShapeDtypeStruct(w.shape, BF16))
    return specs, shapes


def _cast_bands(src_refs, dst_refs):
    for src, dst in zip(src_refs, dst_refs):
        dst[...] = src[...].astype(BF16)


def _adaln_kernel(n_cast, c_ref, w_ref, b_ref, *rest):
    o_ref = rest[n_cast]
    c = c_ref[...]
    c_act = c * jax.nn.sigmoid(c)
    lhs = jnp.concatenate([c_act, jnp.zeros((F32_SUBLANES - c.shape[0], c.shape[1]), F32)], axis=0)
    o_ref[...] = _dot(lhs, w_ref[...]) + b_ref[...]
    _cast_bands(rest[:n_cast], rest[n_cast + 1:])


def _adaln(c, w_ada, b_ada, layer, cast_ws=()):
    b, d = c.shape
    grid = (N_MOD,)
    cast_specs, cast_shapes = _cast_specs(cast_ws, grid)
    return pl.pallas_call(
        functools.partial(_adaln_kernel, len(cast_ws)),
        out_shape=[jax.ShapeDtypeStruct((N_MOD, F32_SUBLANES, d), F32)] + cast_shapes,
        grid=grid,
        in_specs=[
            pl.BlockSpec((b, d), lambda j: (0, 0)),
            pl.BlockSpec((None, d, d), lambda j: (layer, 0, j)),
            pl.BlockSpec((1, d), lambda j: (layer, j)),
        ] + cast_specs,
        out_specs=[pl.BlockSpec((None, F32_SUBLANES, d), lambda j: (j, 0, 0))] + cast_specs,
        compiler_params=_params(1),
        name="adaln",
    )(c, w_ada, b_ada, *cast_ws)


def _mod_row(mod_ref, k):
    return mod_ref[k, pl.ds(pl.program_id(0), 1), :]


def _mod_spec(d):
    return _resident((N_MOD, F32_SUBLANES, d))


def _ffn_kernel(mod_row, n_cast, x_ref, mod_ref, g_ref, wg_ref, wu_ref, wd_ref, *rest):
    o_ref = rest[n_cast]
    shift = _mod_row(mod_ref, mod_row)
    scale = _mod_row(mod_ref, mod_row + 1)
    gate = _mod_row(mod_ref, mod_row + 2)
    for r0 in range(0, x_ref.shape[0], FFN_SUB_ROWS):
        x = x_ref[r0:r0 + FFN_SUB_ROWS, :]
        h = _rms_mod(x, g_ref[...], shift, scale).astype(BF16)
        a = _dot(h, wg_ref[...])
        u = _dot(h, wu_ref[...])
        act = (a * jax.nn.sigmoid(a) * u).astype(BF16)
        y = _dot(act, wd_ref[...])
        o_ref[r0:r0 + FFN_SUB_ROWS, :] = x + 0.5 * gate * y
    _cast_bands(rest[:n_cast], rest[n_cast + 1:])


def _ffn(x, mod, gain, w_gate, w_up, w_down, mod_row, cast_ws=(), tm=1024):
    b, s, d = x.shape
    f = w_gate.shape[1]
    grid = (b, s // tm)
    cast_specs, cast_shapes = _cast_specs(cast_ws, grid)
    row = pl.BlockSpec((None, tm, d), lambda bi, i: (bi, i, 0))
    return pl.pallas_call(
        functools.partial(_ffn_kernel, mod_row, len(cast_ws)),
        out_shape=[jax.ShapeDtypeStruct(x.shape, F32)] + cast_shapes,
        grid=grid,
        in_specs=[
            row,
            _mod_spec(d),
            _resident((1, d)),
            _resident((d, f)),
            _resident((d, f)),
            _resident((f, d)),
        ] + cast_specs,
        out_specs=[row] + cast_specs,
        compiler_params=_params(2),
        name="ffn",
    )(x, mod, gain, w_gate, w_up, w_down, *cast_ws)


def _qkv_col(idx, head):
    group, local = divmod(head, ATTN_HEADS_PER_STEP)
    return ((group * 3 * N_GROUPS + idx) * ATTN_HEADS_PER_STEP + local) * HEAD_DIM


def _regroup_rows(src, dst_f32, dst_bf16, seg):
    q = seg // 4
    for base in range(0, TILE, seg):
        for r in range(4):
            lo = base + r * q
            for j in range(src.shape[0]):
                piece = src[j, pl.ds(base + r, q, stride=4), :]
                if dst_f32 is not None:
                    dst_f32[j, lo:lo + q, :] = piece
                dst_bf16[lo:lo + q, j * HEAD_DIM:(j + 1) * HEAD_DIM] = piece.astype(BF16)


def _proj_kernel(n_cast, x_ref, mod_ref, g_ref, qn_ref, kn_ref, cw_ref, w_ref, *rest):
    cast_in, rest = rest[:n_cast], rest[n_cast:]
    qkv_ref, mix_ref = rest[:2]
    cast_out = rest[2:2 + n_cast]
    scr_a, scr_b, h_scr, xc_scr = rest[2 + n_cast:]

    @pl.when(pl.program_id(1) == 0)
    def _():
        xc_scr[:F32_SUBLANES, :] = jnp.zeros((F32_SUBLANES, xc_scr.shape[1]), xc_scr.dtype)

    d = g_ref.shape[-1]
    aw = HEADS * HEAD_DIM
    subs = tuple(range(0, TILE, PROJ_SUB_ROWS))
    hs = []
    for r0 in subs:
        hf = _rms_mod(x_ref[r0:r0 + PROJ_SUB_ROWS, :], g_ref[...], _mod_row(mod_ref, 3), _mod_row(mod_ref, 4))
        hs.append(hf.astype(BF16))
        for j in range(d // HEAD_DIM):
            scr_a[j, r0:r0 + PROJ_SUB_ROWS, :] = hf[:, j * HEAD_DIM:(j + 1) * HEAD_DIM]
    _regroup_rows(scr_a, scr_b, h_scr.at[0], TILE)
    _regroup_rows(scr_b, None, h_scr.at[1], TILE // 4)
    o = 3 * N_GROUPS * aw
    pad = xc_scr.shape[0] - TILE
    for r0, h in zip(subs, hs):
        rows = slice(r0, r0 + PROJ_SUB_ROWS)
        mix_ref[rows, d:2 * d] = jax.nn.sigmoid(_dot(h, w_ref[:, o + 3 * d:o + 4 * d])).astype(BF16)
        mix_ref[rows, 2 * d:] = jax.nn.sigmoid(_dot(h, w_ref[:, o + 4 * d:o + 5 * d])).astype(BF16)
        xc = _dot(h, w_ref[:, o + 2 * d:o + 3 * d]) * _dot(h, w_ref[:, o:o + d])
        xc_scr[pad + r0:pad + r0 + PROJ_SUB_ROWS, :] = xc
        conv = xc * cw_ref[CONV_K - 1:CONV_K, :]
        for j in range(CONV_K - 1):
            back = CONV_K - 1 - j
            conv = conv + xc_scr[pad + r0 - back:pad + r0 - back + PROJ_SUB_ROWS, :] * cw_ref[j:j + 1, :]
        mix_ref[rows, :d] = (_dot(h, w_ref[:, o + d:o + 2 * d]) * conv).astype(BF16)
    xc_scr[:pad, :] = xc_scr[TILE:, :]
    gains = (qn_ref[...] * (HEAD_DIM ** -0.5 * LOG2_E), kn_ref[...], None)
    for j in range(3):
        for g in range(N_GROUPS):
            c0 = (j * N_GROUPS + g) * aw
            for r0, h in zip(subs, hs):
                lhs = h if g == 0 else h_scr[g - 1, r0:r0 + PROJ_SUB_ROWS, :]
                t = _dot(lhs, w_ref[:, c0:c0 + aw])
                for hd in range(HEADS):
                    col = hd * HEAD_DIM
                    slab = t[:, col:col + HEAD_DIM]
                    if gains[j] is not None:
                        ms = jnp.mean(slab * slab, axis=-1, keepdims=True)
                        slab = slab * lax.rsqrt(ms + EPS) * gains[j]
                    dst = _qkv_col(3 * g + j, hd)
                    qkv_ref[r0:r0 + PROJ_SUB_ROWS, dst:dst + HEAD_DIM] = slab.astype(BF16)
    _cast_bands(cast_in, cast_out)


def _proj(x, mod, gain, q_norm, k_norm, conv_w, w, cast_ws=()):
    b, s, d = x.shape
    aw = HEADS * HEAD_DIM
    grid = (b, s // TILE)
    cast_specs, cast_shapes = _cast_specs(cast_ws, grid)
    row = lambda width: pl.BlockSpec((None, TILE, width), lambda bi, i: (bi, i, 0))
    qkv_w, mix_w = 3 * N_GROUPS * aw, 3 * d
    return pl.pallas_call(
        functools.partial(_proj_kernel, len(cast_ws)),
        out_shape=[jax.ShapeDtypeStruct((b, s, qkv_w), BF16), jax.ShapeDtypeStruct((b, s, mix_w), BF16)] + cast_shapes,
        grid=grid,
        in_specs=[
            row(d),
            _mod_spec(d),
            _resident((1, d)),
            _resident((1, HEAD_DIM)),
            _resident((1, HEAD_DIM)),
            _resident(conv_w.shape),
            _resident(w.shape),
        ] + cast_specs,
        out_specs=[row(qkv_w), row(mix_w)] + cast_specs,
        scratch_shapes=[pltpu.VMEM((d // HEAD_DIM, TILE, HEAD_DIM), F32), pltpu.VMEM((d // HEAD_DIM, TILE, HEAD_DIM), F32),
                        pltpu.VMEM((2, TILE, d), BF16), pltpu.VMEM((F32_SUBLANES + TILE, d), F32)],
        compiler_params=_params(2),
        name="proj",
    )(x, mod, gain, q_norm, k_norm, conv_w, w, *cast_ws)


def _attn_block(q, kcat, vcat, bias):
    s = lax.dot_general(q, kcat, (((1,), (1,)), ((), ())), preferred_element_type=F32) + bias
    m = jnp.max(s, axis=-1, keepdims=True)
    p = jnp.exp2(s - m)
    od = _dot(p.astype(BF16), jnp.concatenate([vcat, jnp.ones_like(vcat)], axis=1))
    return od[:, :HEAD_DIM], jnp.broadcast_to(m, (BAND, HEAD_DIM)), od[:, HEAD_DIM:]


def _band_rows(dil, r, jb):
    per_tile = TILE // dil
    if dil == 1:
        return [(jb * BAND, BAND)]
    pos = r if dil == 4 else (r % 4) * 4 + r // 4
    tiles = BAND // per_tile
    return [((jb * tiles + t) * TILE + pos * per_tile, per_tile) for t in range(tiles)]


def _gather(ref, ranges, lanes):
    parts = [ref[a:a + n, lanes] for a, n in ranges]
    return parts[0] if len(parts) == 1 else jnp.concatenate(parts, axis=0)


def _attn_kernel(qkv_ref, out_ref, *scr):
    per_head = len(scr) // ATTN_HEADS_PER_STEP
    c = pl.program_id(2)

    @pl.when(c == 0)
    def _():
        for hh in range(ATTN_HEADS_PER_STEP):
            for ref in scr[hh * per_head:hh * per_head + 2 * N_GROUPS]:
                ref[...] = jnp.zeros(ref.shape, ref.dtype)

    qi = lax.broadcasted_iota(jnp.int32, (BAND, 2 * BAND), 0)
    kj = lax.broadcasted_iota(jnp.int32, (BAND, 2 * BAND), 1)
    in_prev = (kj < BAND) & (kj >= qi)
    in_cur = (kj >= BAND) & (kj - BAND <= qi)
    neg = jnp.float32(-jnp.inf)
    bias_in = jnp.where(in_prev | in_cur, 0.0, neg)
    bias_first = jnp.where(in_cur, 0.0, neg)
    bias_edge = jnp.where(c > 0, bias_in, bias_first)
    for hh in range(ATTN_HEADS_PER_STEP):
        _attn_head(qkv_ref, out_ref, hh, scr[hh * per_head:(hh + 1) * per_head], bias_in, bias_edge)


def _attn_head(qkv_ref, out_ref, hh, scr, bias_in, bias_edge):
    head_lanes = lambda idx: slice(_qkv_col(idx, hh), _qkv_col(idx, hh) + HEAD_DIM)
    kprev, vprev = scr[0:N_GROUPS], scr[N_GROUPS:2 * N_GROUPS]
    pieces = scr[2 * N_GROUPS:]
    nat = (None,) + tuple(pieces[3 * (g - 1):3 * g] for g in range(1, N_GROUPS))
    tmp = pieces[3 * (N_GROUPS - 1):]
    for g in tuple(range(1, N_GROUPS)) + (0,):
        dil = DILATIONS[g]
        q_l, k_l, v_l = (head_lanes(3 * g + j) for j in range(3))
        blocks = CHUNK // dil // BAND
        carry = [rg for r in range(dil) for rg in _band_rows(dil, r, blocks - 1)]
        for r in range(dil):
            for jb in range(blocks):
                cur = _band_rows(dil, r, jb)
                q = _gather(qkv_ref, cur, q_l)
                if jb == 0:
                    base = r * BAND
                    k_prev = kprev[g][base:base + BAND, :]
                    v_prev = vprev[g][base:base + BAND, :]
                    bias = bias_edge
                else:
                    prev = _band_rows(dil, r, jb - 1)
                    k_prev, v_prev = _gather(qkv_ref, prev, k_l), _gather(qkv_ref, prev, v_l)
                    bias = bias_in
                kcat = jnp.concatenate([k_prev, _gather(qkv_ref, cur, k_l)], axis=0)
                vcat = jnp.concatenate([v_prev, _gather(qkv_ref, cur, v_l)], axis=0)
                block = _attn_block(q, kcat, vcat, bias)
                if dil == 16:
                    rows = pl.ds((r % 4) * (CHUNK // 4) + r // 4, BAND, stride=4)
                    for ref, piece in zip(tmp, block):
                        ref[rows, :] = piece
                elif dil > 1:
                    rows = pl.ds(jb * BAND * dil + r, BAND, stride=dil)
                    for ref, piece in zip(nat[g], block):
                        ref[rows, :] = piece
                else:
                    a = jb * BAND
                    groups = [block] + [[ref[a:a + BAND, :] for ref in nat[gg]] for gg in range(1, N_GROUPS)]
                    m = functools.reduce(jnp.maximum, [mg for _, mg, _ in groups])
                    es = [jnp.exp2(mg - m) for _, mg, _ in groups]
                    num = sum(e * og for e, (og, _, _) in zip(es, groups))
                    den = sum(e * dg for e, (_, _, dg) in zip(es, groups))
                    out_ref[a:a + BAND, hh * HEAD_DIM:(hh + 1) * HEAD_DIM] = (num / den).astype(BF16)
        if dil == 16:
            seg = CHUNK // 4
            for r4 in range(4):
                for dst, src in zip(nat[g], tmp):
                    dst[pl.ds(r4, seg, stride=4), :] = src[r4 * seg:(r4 + 1) * seg, :]
        off = 0
        for a, n in carry:
            kprev[g][off:off + n, :] = qkv_ref[a:a + n, k_l]
            vprev[g][off:off + n, :] = qkv_ref[a:a + n, v_l]
            off += n


def _attn(qkv):
    b, s, _ = qkv.shape
    aw = HEADS * HEAD_DIM
    n_in = 3 * N_GROUPS
    steps_h = HEADS // ATTN_HEADS_PER_STEP
    blk = lambda n_slabs: pl.BlockSpec((None, CHUNK, n_slabs * ATTN_HEADS_PER_STEP * HEAD_DIM),
                                       lambda bi, h, c: (bi, c, h))
    carry_rows = [BAND * dil for dil in DILATIONS]
    head_scratch = ([pltpu.VMEM((n, HEAD_DIM), BF16) for n in carry_rows] * 2
                    + [pltpu.VMEM((CHUNK, HEAD_DIM), F32)] * (3 * N_GROUPS))
    return pl.pallas_call(
        _attn_kernel,
        out_shape=jax.ShapeDtypeStruct((b, s, aw), BF16),
        grid=(b, steps_h, s // CHUNK),
        in_specs=[blk(n_in)],
        out_specs=blk(1),
        scratch_shapes=head_scratch * ATTN_HEADS_PER_STEP,
        compiler_params=_params(3),
        name="attn",
    )(qkv)


def _mixout_kernel(x_ref, mod_ref, o_ref, mix_ref, wa_ref, wc_ref, wo_ref, out_ref):
    d = x_ref.shape[-1]
    for r0 in range(0, x_ref.shape[0], MIX_SUB_ROWS):
        rows = slice(r0, r0 + MIX_SUB_ROWS)
        y_attn = _dot(o_ref[rows, :], wa_ref[...])
        y_conv = _dot(mix_ref[rows, :d], wc_ref[...])
        merged = (mix_ref[rows, d:2 * d].astype(F32) * y_attn + mix_ref[rows, 2 * d:].astype(F32) * y_conv).astype(BF16)
        out_ref[rows, :] = x_ref[rows, :] + _mod_row(mod_ref, 5) * _dot(merged, wo_ref[...])


def _mixout(x, mod, o, mix, w_attn, w_conv, w_out, tm=1024):
    b, s, d = x.shape
    row = lambda width: pl.BlockSpec((None, tm, width), lambda bi, i: (bi, i, 0))
    return pl.pallas_call(
        _mixout_kernel,
        out_shape=jax.ShapeDtypeStruct(x.shape, F32),
        grid=(b, s // tm),
        in_specs=[row(d), _mod_spec(d), row(o.shape[-1]), row(mix.shape[-1]),
                  _resident(w_attn.shape), _resident(w_conv.shape), _resident(w_out.shape)],
        out_specs=row(d),
        compiler_params=_params(2),
        name="mixout",
    )(x, mod, o, mix, w_attn, w_conv, w_out)


def kernel(x, c, w_ada, b_ada, norm_ffn1, ffn1_w_gate, ffn1_w_up, ffn1_w_down, norm_mix, w_in, q_norm, k_norm,
           conv_w, w_attn_branch, w_conv_branch, w_out, norm_ffn2, ffn2_w_gate, ffn2_w_up, ffn2_w_down):
    b, s, d = x.shape
    depth = w_ada.shape[0]
    assert s % CHUNK == 0 and d % HEAD_DIM == 0, "the sequence must be whole CHUNKs, the width whole lane tiles"
    assert b <= F32_SUBLANES, "the modulation rows of a batch must fit one sublane tile"
    assert w_in.shape[-1] == 3 * N_GROUPS * HEADS * HEAD_DIM + 5 * d and conv_w.shape[1] == CONV_K
    for l in range(depth):
        mod, w_gate1, w_up1, w_down1 = _adaln(c, w_ada, b_ada, l,
                                              cast_ws=(ffn1_w_gate[l], ffn1_w_up[l], ffn1_w_down[l]))
        x, w_in_b = _ffn(x, mod, norm_ffn1[l][None, :], w_gate1, w_up1, w_down1, mod_row=0, cast_ws=(w_in[l],))
        qkv, mix, w_attn_b, w_conv_b, w_out_b, w_gate_b, w_up_b, w_down_b = _proj(
            x, mod, norm_mix[l][None, :], q_norm[l][None, :], k_norm[l][None, :], conv_w[l], w_in_b,
            cast_ws=(w_attn_branch[l], w_conv_branch[l], w_out[l], ffn2_w_gate[l], ffn2_w_up[l], ffn2_w_down[l]))
        o = _attn(qkv)
        x = _mixout(x, mod, o, mix, w_attn_b, w_conv_b, w_out_b)
        x, = _ffn(x, mod, norm_ffn2[l][None, :], w_gate_b, w_up_b, w_down_b, mod_row=6)
    return x
```
